```python
import math
import jax, jax.numpy as jnp
from jax import lax
import numpy as np

D_MODEL = 2048
BATCH = 8
SEQ = 2048
DEPTH = 2

A_HEADS = 8
A_HALF_DIM = 64
A_V_DIM = 2 * A_HALF_DIM
B_HEADS = 8
B_NOPE_DIM = 128
B_ROPE_DIM = 64
B_QK_DIM = B_NOPE_DIM + B_ROPE_DIM
B_V_DIM = 128
B_Q_RANK = 512
B_KV_RANK = 256
ROPE_THETA = 10000.0
REL_BUCKETS = 32
REL_MAX_DIST = 128
D_FF = 5632
Q_BLOCK = 128
EPS = 1e-6

N_EVEN = (DEPTH + 1) // 2
N_ODD = DEPTH // 2

A_Q_W = A_HEADS * 2 * A_HALF_DIM
A_K_W = A_HEADS * 2 * A_HALF_DIM
A_V_W = A_HEADS * A_V_DIM
ATTN_IN_W = A_Q_W + A_K_W + A_V_W + B_Q_RANK + B_KV_RANK + B_ROPE_DIM
ATTN_OUT_W = A_HEADS * A_V_DIM + B_HEADS * B_V_DIM
ATTN_SPLITS = [A_Q_W, A_Q_W + A_K_W, A_Q_W + A_K_W + A_V_W,
               A_Q_W + A_K_W + A_V_W + B_Q_RANK,
               A_Q_W + A_K_W + A_V_W + B_Q_RANK + B_KV_RANK]

kernel_name = "hybrid_diffattn_mla_shortconv_encoder"


def rms_norm(x, g):
    xf = x.astype(jnp.float32)
    y = xf * lax.rsqrt(jnp.mean(xf * xf, axis=-1, keepdims=True) + EPS)
    return (y * g.astype(jnp.float32)).astype(x.dtype)


def dwconv3(x, w, b=None):
    xp = jnp.pad(x, ((0, 0), (1, 1), (0, 0)))
    y = w[0] * xp[:, :-2] + w[1] * xp[:, 1:-1] + w[2] * xp[:, 2:]
    return y if b is None else y + b


def t5_bucket(rel):
    nb = REL_BUCKETS // 2
    max_exact = nb // 2
    bucket = jnp.where(rel > 0, nb, 0).astype(jnp.int32)
    n = jnp.abs(rel)
    nf = jnp.maximum(n, max_exact).astype(jnp.float32)
    large = max_exact + (jnp.log(nf / max_exact) / math.log(REL_MAX_DIST / max_exact)
                         * (nb - max_exact)).astype(jnp.int32)
    large = jnp.minimum(large, nb - 1)
    return bucket + jnp.where(n < max_exact, n, large)


def rope_cos_sin(positions):
    inv = 1.0 / (ROPE_THETA ** (jnp.arange(0, B_ROPE_DIM, 2, dtype=jnp.float32) / B_ROPE_DIM))
    ang = positions.astype(jnp.float32)[..., None] * inv
    return jnp.cos(ang), jnp.sin(ang)


def apply_rope(x, cos, sin):
    x1, x2 = jnp.split(x.astype(jnp.float32), 2, axis=-1)
    return jnp.concatenate([x1 * cos - x2 * sin, x2 * cos + x1 * sin], axis=-1).astype(x.dtype)


def diff_attention(q, k, v, positions, rel_table, lam):
    b, h, _, s, dh = q.shape
    nblk = s // Q_BLOCK
    qb = jnp.moveaxis(q.reshape(b, h, 2, nblk, Q_BLOCK, dh), 3, 0)
    starts = jnp.arange(nblk, dtype=jnp.int32) * Q_BLOCK
    scale = dh ** -0.5

    def one_block(args):
        q_blk, start = args
        q_pos = lax.dynamic_slice_in_dim(positions, start, Q_BLOCK, axis=1)
        rel = positions[:, None, :] - q_pos[:, :, None]
        bias = jnp.moveaxis(rel_table[t5_bucket(rel)], -1, 1)
        logits = (jnp.einsum('bhjqd,bhjkd->bhjqk', q_blk, k).astype(jnp.float32) * scale
                  + bias[:, :, None].astype(jnp.float32))
        p = jax.nn.softmax(logits, axis=-1)
        w = p[:, :, 0] - lam * p[:, :, 1]
        return jnp.einsum('bhqk,bhkd->bhqd', w.astype(v.dtype), v)

    out = lax.map(one_block, (qb, starts))
    return jnp.moveaxis(out, 0, 2).reshape(b, h, s, -1)


def mla_attention(q, k, v):
    b, h, s, dqk = q.shape
    nblk = s // Q_BLOCK
    qb = jnp.moveaxis(q.reshape(b, h, nblk, Q_BLOCK, dqk), 2, 0)
    scale = dqk ** -0.5

    def one_block(q_blk):
        logits = jnp.einsum('bhqd,bhkd->bhqk', q_blk, k).astype(jnp.float32) * scale
        p = jax.nn.softmax(logits, axis=-1)
        return jnp.einsum('bhqk,bhkd->bhqd', p.astype(v.dtype), v)

    out = lax.map(one_block, qb)
    return jnp.moveaxis(out, 0, 2).reshape(b, h, s, -1)


def attn_mixer(h, positions, cos, sin, rel_table, layer_idx, w_in, dq_g, dk_g,
               lq1, lk1, lq2, lk2, subln_g, q_a_g, w_uq, kv_a_g, w_ukv, mq_g, mk_g, w_out):
    b, s, _ = h.shape
    aq, ak, av, cq, ckv, kr = jnp.split(h @ w_in, ATTN_SPLITS, axis=-1)

    aq = rms_norm(aq.reshape(b, s, A_HEADS, 2, A_HALF_DIM), dq_g).transpose(0, 2, 3, 1, 4)
    ak = rms_norm(ak.reshape(b, s, A_HEADS, 2, A_HALF_DIM), dk_g).transpose(0, 2, 3, 1, 4)
    av = av.reshape(b, s, A_HEADS, A_V_DIM).transpose(0, 2, 1, 3)
    lam_init = 0.8 - 0.6 * math.exp(-0.3 * layer_idx)
    lam = (jnp.exp(jnp.sum(lq1.astype(jnp.float32) * lk1.astype(jnp.float32)))
           - jnp.exp(jnp.sum(lq2.astype(jnp.float32) * lk2.astype(jnp.float32))) + lam_init)
    oa = diff_attention(aq, ak, av, positions, rel_table, lam)
    oa = rms_norm(oa, subln_g) * (1.0 - lam_init)
    oa = oa.transpose(0, 2, 1, 3).reshape(b, s, A_HEADS * A_V_DIM)

    q = (rms_norm(cq, q_a_g) @ w_uq).reshape(b, s, B_HEADS, B_QK_DIM)
    q = rms_norm(q, mq_g)
    q = jnp.concatenate([q[..., :B_NOPE_DIM],
                         apply_rope(q[..., B_NOPE_DIM:], cos[:, :, None], sin[:, :, None])], axis=-1)
    kv = (rms_norm(ckv, kv_a_g) @ w_ukv).reshape(b, s, B_HEADS, B_NOPE_DIM + B_V_DIM)
    k_nope, vb = jnp.split(kv, [B_NOPE_DIM], axis=-1)
    k_rope = jnp.broadcast_to(kr[:, :, None, :], (b, s, B_HEADS, B_ROPE_DIM))
    k = rms_norm(jnp.concatenate([k_nope, k_rope], axis=-1), mk_g)
    k = jnp.concatenate([k[..., :B_NOPE_DIM],
                         apply_rope(k[..., B_NOPE_DIM:], cos[:, :, None], sin[:, :, None])], axis=-1)
    ob = mla_attention(q.transpose(0, 2, 1, 3), k.transpose(0, 2, 1, 3), vb.transpose(0, 2, 1, 3))
    ob = ob.transpose(0, 2, 1, 3).reshape(b, s, B_HEADS * B_V_DIM)

    return jnp.concatenate([oa, ob], axis=-1) @ w_out


def short_conv_mixer(h, w_in, conv_w, w_out):
    bg, cg, hv = jnp.split(h @ w_in, 3, axis=-1)
    return (bg * dwconv3(cg * hv, conv_w)) @ w_out


def conv_ffn(h, w_gate, w_up, dw_w, dw_b, w_down):
    g = dwconv3(h @ w_gate, dw_w, dw_b)
    return (jax.nn.silu(g) * (h @ w_up)) @ w_down


def setup_inputs(seed: int = 0) -> dict:
    key = jax.random.key(seed)
    ks = iter(jax.random.split(key, 40))

    def dense(shape):
        return jax.random.normal(next(ks), shape, jnp.float32) * (shape[-2] ** -0.5)

    def gain(shape):
        return 1.0 + 0.02 * jax.random.normal(next(ks), shape, jnp.float32)

    def small(shape, scale):
        return scale * jax.random.normal(next(ks), shape, jnp.float32)

    x = jax.random.normal(next(ks), (BATCH, SEQ, D_MODEL), jnp.float32)
    positions = (jnp.arange(SEQ, dtype=jnp.int32)[None, :]
                 + jax.random.randint(next(ks), (BATCH, 1), 0, SEQ, dtype=jnp.int32))
    return {
        "x": x,
        "positions": positions,
        "rel_bias_table": small((REL_BUCKETS, A_HEADS), 0.2),
        "attn_norm_g": gain((N_EVEN, D_MODEL)),
        "attn_w_in": dense((N_EVEN, D_MODEL, ATTN_IN_W)),
        "diff_q_norm_g": gain((N_EVEN, A_HALF_DIM)),
        "diff_k_norm_g": gain((N_EVEN, A_HALF_DIM)),
        "diff_lambda_q1": small((N_EVEN, A_HALF_DIM), 0.1),
        "diff_lambda_k1": small((N_EVEN, A_HALF_DIM), 0.1),
        "diff_lambda_q2": small((N_EVEN, A_HALF_DIM), 0.1),
        "diff_lambda_k2": small((N_EVEN, A_HALF_DIM), 0.1),
        "diff_subln_g": gain((N_EVEN, A_V_DIM)),
        "mla_q_a_norm_g": gain((N_EVEN, B_Q_RANK)),
        "mla_w_uq": dense((N_EVEN, B_Q_RANK, B_HEADS * B_QK_DIM)),
        "mla_kv_a_norm_g": gain((N_EVEN, B_KV_RANK)),
        "mla_w_ukv": dense((N_EVEN, B_KV_RANK, B_HEADS * (B_NOPE_DIM + B_V_DIM))),
        "mla_q_norm_g": gain((N_EVEN, B_QK_DIM)),
        "mla_k_norm_g": gain((N_EVEN, B_QK_DIM)),
        "attn_w_out": dense((N_EVEN, ATTN_OUT_W, D_MODEL)),
        "conv_norm_g": gain((N_ODD, D_MODEL)),
        "conv_w_in": dense((N_ODD, D_MODEL, 3 * D_MODEL)),
        "conv_w": jax.random.normal(next(ks), (N_ODD, 3, D_MODEL), jnp.float32) * (3 ** -0.5),
        "conv_w_out": dense((N_ODD, D_MODEL, D_MODEL)),
        "ffn_norm_g": gain((DEPTH, D_MODEL)),
        "ffn_w_gate": dense((DEPTH, D_MODEL, D_FF)),
        "ffn_w_up": dense((DEPTH, D_MODEL, D_FF)),
        "ffn_dwconv_w": jax.random.normal(next(ks), (DEPTH, 3, D_FF), jnp.float32) * (3 ** -0.5),
        "ffn_dwconv_b": small((DEPTH, D_FF), 0.02),
        "ffn_w_down": dense((DEPTH, D_FF, D_MODEL)),
    }


def reference(x, positions, rel_bias_table, attn_norm_g, attn_w_in, diff_q_norm_g, diff_k_norm_g,
              diff_lambda_q1, diff_lambda_k1, diff_lambda_q2, diff_lambda_k2, diff_subln_g,
              mla_q_a_norm_g, mla_w_uq, mla_kv_a_norm_g, mla_w_ukv, mla_q_norm_g, mla_k_norm_g,
              attn_w_out, conv_norm_g, conv_w_in, conv_w, conv_w_out, ffn_norm_g, ffn_w_gate,
              ffn_w_up, ffn_dwconv_w, ffn_dwconv_b, ffn_w_down):
    cos, sin = rope_cos_sin(positions)
    for layer in range(DEPTH):
        i = layer // 2
        if layer % 2 == 0:
            x = x + attn_mixer(rms_norm(x, attn_norm_g[i]), positions, cos, sin, rel_bias_table, layer,
                               attn_w_in[i], diff_q_norm_g[i], diff_k_norm_g[i],
                               diff_lambda_q1[i], diff_lambda_k1[i], diff_lambda_q2[i], diff_lambda_k2[i],
                               diff_subln_g[i], mla_q_a_norm_g[i], mla_w_uq[i], mla_kv_a_norm_g[i],
                               mla_w_ukv[i], mla_q_norm_g[i], mla_k_norm_g[i], attn_w_out[i])
        else:
            x = x + short_conv_mixer(rms_norm(x, conv_norm_g[i]), conv_w_in[i], conv_w[i], conv_w_out[i])
        x = x + conv_ffn(rms_norm(x, ffn_norm_g[layer]), ffn_w_gate[layer], ffn_w_up[layer],
                         ffn_dwconv_w[layer], ffn_dwconv_b[layer], ffn_w_down[layer])
    return x
```

```python
import functools
import math

import jax
import jax.numpy as jnp
from jax import lax
from jax.experimental import pallas as pl
from jax.experimental.pallas import tpu as pltpu

D_MODEL = 2048
A_HEADS = 8
A_HALF_DIM = 64
A_V_DIM = 128
B_HEADS = 8
B_NOPE_DIM = 128
B_ROPE_DIM = 64
B_QK_DIM = B_NOPE_DIM + B_ROPE_DIM
B_V_DIM = 128
B_Q_RANK = 512
B_KV_RANK = 256
B_QK_PAD = 256
ROPE_THETA = 10000.0
REL_BUCKETS = 32
D_FF = 5632
EPS = 1e-6
A_W = A_HEADS * 2 * A_HALF_DIM
LAT_W = B_Q_RANK + B_KV_RANK + 128

LANES = 128
HALO = 16
BIAS_TILE = 128
VMEM_LIMIT = 56 * 1024 * 1024

_F32 = jnp.float32
_BF16 = jnp.bfloat16


def _rms(xf, g):
    ms = jnp.mean(xf * xf, axis=-1, keepdims=True)
    return xf * lax.rsqrt(ms + EPS) * g


def _dot(a, b):
    return jnp.dot(a, b, preferred_element_type=_F32)


def _dot_nt(a, b):
    return lax.dot_general(a, b, (((1,), (1,)), ((), ())), preferred_element_type=_F32)


def _params(sem):
    return pltpu.CompilerParams(dimension_semantics=sem, vmem_limit_bytes=VMEM_LIMIT)


def _proj_a_kernel(x_ref, g_ref, w_ref, qg_ref, kg_ref, bd_ref, o_ref, hn_ref, *, tn):
    j = pl.program_id(1)
    qk_tiles = 2 * A_W // tn

    @pl.when(j == 0)
    def _():
        hn_ref[...] = _rms(x_ref[...], g_ref[...]).astype(_BF16)

    y = _dot(hn_ref[...], w_ref[...])

    @pl.when(j < qk_tiles)
    def _():
        gain = jnp.where(j < qk_tiles // 2, qg_ref[...], kg_ref[...])
        ms = _dot((y * y).astype(_BF16), bd_ref[...]) * (1.0 / A_HALF_DIM)
        o_ref[...] = (y * lax.rsqrt(ms + EPS) * gain).astype(_BF16)

    @pl.when(j >= qk_tiles)
    def _():
        o_ref[...] = y.astype(_BF16)


def _proj_a(x, g, w, qg, kg, *, tm=512, tn=512):
    t = x.shape[0]
    n = w.shape[1]
    grp = jnp.arange(tn) // A_HALF_DIM
    bd = (grp[:, None] == grp[None, :]).astype(_BF16)
    return pl.pallas_call(
        functools.partial(_proj_a_kernel, tn=tn),
        grid=(t // tm, n // tn),
        in_specs=[
            pl.BlockSpec((tm, D_MODEL), lambda i, j: (i, 0)),
            pl.BlockSpec((1, D_MODEL), lambda i, j: (0, 0)),
            pl.BlockSpec((D_MODEL, tn), lambda i, j: (0, j)),
            pl.BlockSpec((1, tn), lambda i, j: (0, 0)),
            pl.BlockSpec((1, tn), lambda i, j: (0, 0)),
            pl.BlockSpec((tn, tn), lambda i, j: (0, 0)),
        ],
        out_specs=pl.BlockSpec((tm, tn), lambda i, j: (i, j)),
        out_shape=jax.ShapeDtypeStruct((t, n), _BF16),
        scratch_shapes=[pltpu.VMEM((tm, D_MODEL), _BF16)],
        compiler_params=_params(("parallel", "arbitrary")),
        name="attn_proj_a",
    )(x, g, w, qg, kg, bd)


def _mla_prep_kernel(x_ref, pos_ref, inv_ref, g_ref, wl_ref, qag_ref, wuq_ref, kvag_ref, wukv_ref,
                     mqg_ref, mkg_ref, q_ref, k_ref, v_ref):
    hn = _rms(x_ref[...], g_ref[...]).astype(_BF16)
    lat = _dot(hn, wl_ref[...])
    cq = lat[:, :B_Q_RANK]
    ckv = lat[:, B_Q_RANK:B_Q_RANK + B_KV_RANK]
    kr = lat[:, B_Q_RANK + B_KV_RANK:]

    ang = pos_ref[...].astype(_F32) * inv_ref[...]
    lane = lax.broadcasted_iota(jnp.int32, (1, LANES), 1)
    half = B_ROPE_DIM // 2
    cosv = jnp.cos(ang)
    sinv = jnp.sin(ang)
    c_tab = jnp.where(lane < B_ROPE_DIM, cosv, 0.0)
    s_lo = jnp.where(lane < half, -sinv, 0.0)
    s_hi = jnp.where((lane >= half) & (lane < B_ROPE_DIM), sinv, 0.0)

    def rope(r):
        return r * c_tab + pltpu.roll(r, LANES - half, 1) * s_lo + pltpu.roll(r, half, 1) * s_hi

    scale = B_QK_DIM ** -0.5
    qf = _dot(_rms(cq, qag_ref[...]).astype(_BF16), wuq_ref[...])
    mqg = mqg_ref[...]
    for h in range(B_HEADS):
        slab = qf[:, h * B_QK_PAD:(h + 1) * B_QK_PAD]
        ms = jnp.sum(slab * slab, axis=-1, keepdims=True) * (1.0 / B_QK_DIM)
        sn = slab * (lax.rsqrt(ms + EPS) * scale) * mqg
        q_ref[:, h * B_QK_PAD:h * B_QK_PAD + B_NOPE_DIM] = sn[:, :B_NOPE_DIM].astype(_BF16)
        q_ref[:, h * B_QK_PAD + B_NOPE_DIM:(h + 1) * B_QK_PAD] = rope(sn[:, B_NOPE_DIM:]).astype(_BF16)

    kv = _dot(_rms(ckv, kvag_ref[...]).astype(_BF16), wukv_ref[...])
    mkg = mkg_ref[...]
    kr_ss = jnp.sum(kr * kr, axis=-1, keepdims=True)
    kr_rot = rope(kr * mkg[:, B_NOPE_DIM:])
    for h in range(B_HEADS):
        kn = kv[:, h * 256:h * 256 + B_NOPE_DIM]
        ms = (jnp.sum(kn * kn, axis=-1, keepdims=True) + kr_ss) * (1.0 / B_QK_DIM)
        rs = lax.rsqrt(ms + EPS)
        k_ref[:, h * B_QK_PAD:h * B_QK_PAD + B_NOPE_DIM] = (kn * rs * mkg[:, :B_NOPE_DIM]).astype(_BF16)
        k_ref[:, h * B_QK_PAD + B_NOPE_DIM:(h + 1) * B_QK_PAD] = (kr_rot * rs).astype(_BF16)
        v_ref[:, h * B_V_DIM:(h + 1) * B_V_DIM] = kv[:, h * 256 + B_NOPE_DIM:(h + 1) * 256].astype(_BF16)


def _mla_prep(x, pos, inv, g, wl, qag, wuq, kvag, wukv, mqg, mkg, *, tm=512):
    t = x.shape[0]
    full = lambda a: pl.BlockSpec(a.shape, lambda i: (0,) * a.ndim)
    return pl.pallas_call(
        _mla_prep_kernel,
        grid=(t // tm,),
        in_specs=[
            pl.BlockSpec((tm, D_MODEL), lambda i: (i, 0)),
            pl.BlockSpec((tm, 1), lambda i: (i, 0)),
            full(inv), full(g), full(wl), full(qag), full(wuq), full(kvag), full(wukv), full(mqg), full(mkg),
        ],
        out_specs=[
            pl.BlockSpec((tm, B_HEADS * B_QK_PAD), lambda i: (i, 0)),
            pl.BlockSpec((tm, B_HEADS * B_QK_PAD), lambda i: (i, 0)),
            pl.BlockSpec((tm, B_HEADS * B_V_DIM), lambda i: (i, 0)),
        ],
        out_shape=[
            jax.ShapeDtypeStruct((t, B_HEADS * B_QK_PAD), _BF16),
            jax.ShapeDtypeStruct((t, B_HEADS * B_QK_PAD), _BF16),
            jax.ShapeDtypeStruct((t, B_HEADS * B_V_DIM), _BF16),
        ],
        compiler_params=_params(("parallel",)),
        name="mla_prep",
    )(x, pos, inv, g, wl, qag, wuq, kvag, wukv, mqg, mkg)


def _t5_bucket_of(rel):
    nb = REL_BUCKETS // 2
    max_exact = nb // 2
    n = jnp.abs(rel)
    large = jnp.full(rel.shape, max_exact, jnp.int32)
    for k in range(1, nb - max_exact):
        large = large + (n >= math.ceil(max_exact * 2.0 ** (k / 2.0))).astype(jnp.int32)
    return jnp.where(rel > 0, nb, 0) + jnp.where(n < max_exact, n, large)


def _bias_tiles_kernel(tbl_ref, o_ref):
    h = pl.program_id(0)
    row = lax.broadcasted_iota(jnp.int32, (BIAS_TILE, BIAS_TILE), 0)
    col = lax.broadcasted_iota(jnp.int32, (BIAS_TILE, BIAS_TILE), 1)
    for d in range(-2, 3):
        if abs(d) == 2:
            rel = jnp.full((BIAS_TILE, BIAS_TILE), d * BIAS_TILE // 2, jnp.int32)
        else:
            rel = d * BIAS_TILE + col - row
        bucket = _t5_bucket_of(rel)
        val = jnp.zeros((BIAS_TILE, BIAS_TILE), _F32)
        for c in range(REL_BUCKETS):
            val = jnp.where(bucket == c, tbl_ref[c * A_HEADS + h], val)
        o_ref[0, d + 2] = val


def _bias_tiles(rel_table):
    return pl.pallas_call(
        _bias_tiles_kernel,
        grid=(A_HEADS,),
        in_specs=[pl.BlockSpec(memory_space=pltpu.SMEM)],
        out_specs=pl.BlockSpec((1, 5, BIAS_TILE, BIAS_TILE), lambda h: (h, 0, 0, 0)),
        out_shape=jax.ShapeDtypeStruct((A_HEADS, 5, BIAS_TILE, BIAS_TILE), _F32),
        compiler_params=_params(("arbitrary",)),
        name="rel_bias_tiles",
    )(rel_table.reshape(-1))


def _softmax_parts(s):
    m = jnp.max(s, axis=-1, keepdims=True)
    e = jnp.exp(s - m)
    return e, jnp.sum(e, axis=-1, keepdims=True)


def _diff_attn_kernel(q_ref, k_ref, v_ref, bias_ref, lq1_ref, lk1_ref, lq2_ref, lk2_ref, sg_ref, o_ref,
                      *, tq, seq, lam_init):
    qi = pl.program_id(2)
    q = q_ref[...]
    k = k_ref[...]
    v = v_ref[...]
    lane = lax.broadcasted_iota(jnp.int32, (1, LANES), 1)
    zero = jnp.zeros_like(q)
    s1 = _dot_nt(jnp.where(lane < A_HALF_DIM, q, zero), k)
    s2 = _dot_nt(jnp.where(lane >= A_HALF_DIM, q, zero), k)

    nkb = seq // BIAS_TILE
    rows = []
    for qs in range(tq // BIAS_TILE):
        qb = qi * (tq // BIAS_TILE) + qs
        rows.append(jnp.concatenate(
            [bias_ref[0, jnp.clip(kb - qb, -2, 2) + 2] for kb in range(nkb)], axis=1))
    bias = jnp.concatenate(rows, axis=0) if len(rows) > 1 else rows[0]

    e1, l1 = _softmax_parts(s1 + bias)
    e2, l2 = _softmax_parts(s2 + bias)
    lam = (jnp.exp(jnp.sum(lq1_ref[...] * lk1_ref[...], axis=-1, keepdims=True))
           - jnp.exp(jnp.sum(lq2_ref[...] * lk2_ref[...], axis=-1, keepdims=True)) + lam_init)
    o = _dot(e1.astype(_BF16), v) / l1 - lam * (_dot(e2.astype(_BF16), v) / l2)
    o_ref[...] = (_rms(o, sg_ref[...]) * (1.0 - lam_init)).astype(_BF16)


def _diff_attn(qkv, bias_tiles, lq1, lk1, lq2, lk2, sg, *, batch, seq, lam_init, tq=256):
    nq = seq // tq
    small = lambda a: pl.BlockSpec(a.shape, lambda b, h, i: (0,) * a.ndim)
    return pl.pallas_call(
        functools.partial(_diff_attn_kernel, tq=tq, seq=seq, lam_init=lam_init),
        grid=(batch, A_HEADS, nq),
        in_specs=[
            pl.BlockSpec((tq, LANES), lambda b, h, i: (b * nq + i, h)),
            pl.BlockSpec((seq, LANES), lambda b, h, i: (b, A_HEADS + h)),
            pl.BlockSpec((seq, LANES), lambda b, h, i: (b, 2 * A_HEADS + h)),
            pl.BlockSpec((1, 5, BIAS_TILE, BIAS_TILE), lambda b, h, i: (h, 0, 0, 0)),
            small(lq1), small(lk1), small(lq2), small(lk2), small(sg),
        ],
        out_specs=pl.BlockSpec((tq, A_V_DIM), lambda b, h, i: (b * nq + i, h)),
        out_shape=jax.ShapeDtypeStruct((batch * seq, A_HEADS * A_V_DIM), _BF16),
        compiler_params=_params(("parallel", "parallel", "arbitrary")),
        name="diff_attn",
    )(qkv, qkv, qkv, bias_tiles, lq1, lk1, lq2, lk2, sg)


def _mla_attn_kernel(q_ref, k_ref, v_ref, o_ref):
    e, l = _softmax_parts(_dot_nt(q_ref[...], k_ref[...]))
    o_ref[...] = (_dot(e.astype(_BF16), v_ref[...]) / l).astype(_BF16)


def _mla_attn(q, k, v, *, batch, seq, tq=256):
    nq = seq // tq
    return pl.pallas_call(
        _mla_attn_kernel,
        grid=(batch, B_HEADS, nq),
        in_specs=[
            pl.BlockSpec((tq, B_QK_PAD), lambda b, h, i: (b * nq + i, h)),
            pl.BlockSpec((seq, B_QK_PAD), lambda b, h, i: (b, h)),
            pl.BlockSpec((seq, B_V_DIM), lambda b, h, i: (b, h)),
        ],
        out_specs=pl.BlockSpec((tq, B_V_DIM), lambda b, h, i: (b * nq + i, h)),
        out_shape=jax.ShapeDtypeStruct((batch * seq, B_HEADS * B_V_DIM), _BF16),
        compiler_params=_params(("parallel", "parallel", "arbitrary")),
        name="mla_attn",
    )(q, k, v)


def _out_proj_kernel(x_ref, oa_ref, ob_ref, wa_ref, wb_ref, o_ref):
    o_ref[...] = x_ref[...] + _dot(oa_ref[...], wa_ref[...]) + _dot(ob_ref[...], wb_ref[...])


def _out_proj(x, oa, ob, wa, wb, *, tm=512):
    t = x.shape[0]
    return pl.pallas_call(
        _out_proj_kernel,
        grid=(t // tm,),
        in_specs=[
            pl.BlockSpec((tm, D_MODEL), lambda i: (i, 0)),
            pl.BlockSpec((tm, oa.shape[1]), lambda i: (i, 0)),
            pl.BlockSpec((tm, ob.shape[1]), lambda i: (i, 0)),
            pl.BlockSpec(wa.shape, lambda i: (0, 0)),
            pl.BlockSpec(wb.shape, lambda i: (0, 0)),
        ],
        out_specs=pl.BlockSpec((tm, D_MODEL), lambda i: (i, 0)),
        out_shape=jax.ShapeDtypeStruct((t, D_MODEL), _F32),
        compiler_params=_params(("parallel",)),
        name="attn_out_proj",
    )(x, oa, ob, wa, wb)


def _dwconv3(g_ext, cw, tm):
    n = g_ext.shape[0]
    prev = pltpu.roll(g_ext, 1, 0)[HALO:HALO + tm]
    nxt = pltpu.roll(g_ext, n - 1, 0)[HALO:HALO + tm]
    return cw[0:1] * prev + cw[1:2] * g_ext[HALO:HALO + tm] + cw[2:3] * nxt


def _gated_block_kernel(*refs, kind, tm, seq):
    if kind == "ffn":
        x_ref, xp_ref, xn_ref, g_ref, wa_ref, wb_ref, cw_ref, cb_ref, wd_ref, o_ref, hn_ref = refs
    else:
        x_ref, xp_ref, xn_ref, g_ref, wa_ref, wb_ref, wc_ref, cw_ref, wd_ref, o_ref, hn_ref = refs
    i = pl.program_id(0)
    j = pl.program_id(1)
    tiles_per_seq = seq // tm

    @pl.when(j == 0)
    def _():
        g = g_ref[...]
        pos_in_seq = i % tiles_per_seq
        keep_prev = (pos_in_seq != 0).astype(_F32)
        keep_next = (pos_in_seq != tiles_per_seq - 1).astype(_F32)
        hn_ref[0:HALO] = (_rms(xp_ref[...], g) * keep_prev).astype(_BF16)
        hn_ref[HALO:HALO + tm] = _rms(x_ref[...], g).astype(_BF16)
        hn_ref[HALO + tm:] = (_rms(xn_ref[...], g) * keep_next).astype(_BF16)
        o_ref[...] = x_ref[...]

    h_ext = hn_ref[...]
    h_main = hn_ref[HALO:HALO + tm]
    cw = cw_ref[...]
    if kind == "ffn":
        c = _dwconv3(_dot(h_ext, wa_ref[...]), cw, tm) + cb_ref[...]
        act = c * jax.nn.sigmoid(c) * _dot(h_main, wb_ref[...])
    else:
        act = _dot(h_main, wa_ref[...]) * _dwconv3(_dot(h_ext, wb_ref[...]) * _dot(h_ext, wc_ref[...]), cw, tm)
    o_ref[...] += _dot(act.astype(_BF16), wd_ref[...])


def _gated_block(kind, x, g, ups, cw, cb, wd, *, seq, tm=512, tc=512):
    t = x.shape[0]
    c_total = wd.shape[0]
    nc = c_total // tc
    hb = tm // HALO
    row_specs = [
        pl.BlockSpec((tm, D_MODEL), lambda i, j: (i, 0)),
        pl.BlockSpec((HALO, D_MODEL), lambda i, j: (jnp.maximum(i * hb - 1, 0), 0)),
        pl.BlockSpec((HALO, D_MODEL), lambda i, j: (jnp.minimum((i + 1) * hb, t // HALO - 1), 0)),
        pl.BlockSpec((1, D_MODEL), lambda i, j: (0, 0)),
    ]
    if kind == "ffn":
        wg, wu = ups
        args = (x, x, x, g, wg, wu, cw, cb, wd)
        w_specs = [
            pl.BlockSpec((D_MODEL, tc), lambda i, j: (0, j)),
            pl.BlockSpec((D_MODEL, tc), lambda i, j: (0, j)),
            pl.BlockSpec((3, tc), lambda i, j: (0, j)),
            pl.BlockSpec((1, tc), lambda i, j: (0, j)),
        ]
    else:
        (w_in,) = ups
        args = (x, x, x, g, w_in, w_in, w_in, cw, wd)
        w_specs = [
            pl.BlockSpec((D_MODEL, tc), lambda i, j: (0, j)),
            pl.BlockSpec((D_MODEL, tc), lambda i, j: (0, nc + j)),
            pl.BlockSpec((D_MODEL, tc), lambda i, j: (0, 2 * nc + j)),
            pl.BlockSpec((3, tc), lambda i, j: (0, j)),
        ]
    return pl.pallas_call(
        functools.partial(_gated_block_kernel, kind=kind, tm=tm, seq=seq),
        grid=(t // tm, nc),
        in_specs=row_specs + w_specs + [pl.BlockSpec((tc, D_MODEL), lambda i, j: (j, 0))],
        out_specs=pl.BlockSpec((tm, D_MODEL), lambda i, j: (i, 0)),
        out_shape=jax.ShapeDtypeStruct((t, D_MODEL), _F32),
        scratch_shapes=[pltpu.VMEM((tm + 2 * HALO, D_MODEL), _BF16)],
        compiler_params=_params(("parallel", "arbitrary")),
        name=kind + "_block",
    )(*args)


def _pad_lanes(a, width):
    return jnp.pad(a, ((0, 0), (0, width - a.shape[1])))


def _attn_layer(x, pos_col, rel_table, layer_idx, norm_g, w_in, dq_g, dk_g, lq1, lk1, lq2, lk2, subln_g,
                q_a_g, w_uq, kv_a_g, w_ukv, mq_g, mk_g, w_out, *, batch, seq):
    row = lambda a: a.reshape(1, -1)
    lam_init = 0.8 - 0.6 * math.exp(-0.3 * layer_idx)

    qg = jnp.tile(dq_g, 512 // A_HALF_DIM).reshape(1, -1) * (A_HALF_DIM ** -0.5)
    kg = jnp.tile(dk_g, 512 // A_HALF_DIM).reshape(1, -1)
    qkv = _proj_a(x, row(norm_g), w_in[:, :3 * A_W].astype(_BF16), qg, kg)

    w_lat = _pad_lanes(w_in[:, 3 * A_W:], LAT_W).astype(_BF16)
    w_uq_p = jnp.pad(w_uq.reshape(B_Q_RANK, B_HEADS, B_QK_DIM),
                     ((0, 0), (0, 0), (0, B_QK_PAD - B_QK_DIM))).reshape(B_Q_RANK, -1).astype(_BF16)
    half = B_ROPE_DIM // 2
    inv = 1.0 / (ROPE_THETA ** (jnp.arange(0, B_ROPE_DIM, 2, dtype=_F32) / B_ROPE_DIM))
    inv_lanes = _pad_lanes(jnp.tile(inv, 2).reshape(1, -1), LANES)
    qm, km, vm = _mla_prep(x, pos_col, inv_lanes, row(norm_g), w_lat, row(q_a_g), w_uq_p, row(kv_a_g),
                           w_ukv.astype(_BF16), _pad_lanes(row(mq_g), B_QK_PAD), _pad_lanes(row(mk_g), B_QK_PAD))
    del half

    oa = _diff_attn(qkv, _bias_tiles(rel_table), row(lq1), row(lk1), row(lq2), row(lk2), row(subln_g),
                    batch=batch, seq=seq, lam_init=lam_init)
    ob = _mla_attn(qm, km, vm, batch=batch, seq=seq)
    w_out_b = w_out.astype(_BF16)
    return _out_proj(x, oa, ob, w_out_b[:A_HEADS * A_V_DIM], w_out_b[A_HEADS * A_V_DIM:])


def kernel(x, positions, rel_bias_table, attn_norm_g, attn_w_in, diff_q_norm_g, diff_k_norm_g, diff_lambda_q1, diff_lambda_k1, diff_lambda_q2, diff_lambda_k2, diff_subln_g, mla_q_a_norm_g, mla_w_uq, mla_kv_a_norm_g, mla_w_ukv, mla_q_norm_g, mla_k_norm_g, attn_w_out, conv_norm_g, conv_w_in, conv_w, conv_w_out, ffn_norm_g, ffn_w_gate, ffn_w_up, ffn_dwconv_w, ffn_dwconv_b, ffn_w_down):
    batch, seq, d = x.shape
    depth = ffn_norm_g.shape[0]
    h = x.reshape(batch * seq, d)
    pos_col = positions.reshape(batch * seq, 1)
    for layer in range(depth):
        i = layer // 2
        if layer % 2 == 0:
            h = _attn_layer(h, pos_col, rel_bias_table, layer, attn_norm_g[i], attn_w_in[i], diff_q_norm_g[i],
                            diff_k_norm_g[i], diff_lambda_q1[i], diff_lambda_k1[i], diff_lambda_q2[i],
                            diff_lambda_k2[i], diff_subln_g[i], mla_q_a_norm_g[i], mla_w_uq[i],
                            mla_kv_a_norm_g[i], mla_w_ukv[i], mla_q_norm_g[i], mla_k_norm_g[i], attn_w_out[i],
                            batch=batch, seq=seq)
        else:
            h = _gated_block("conv", h, conv_norm_g[i].reshape(1, -1), (conv_w_in[i].astype(_BF16),),
                             conv_w[i], None, conv_w_out[i].astype(_BF16), seq=seq)
        h = _gated_block("ffn", h, ffn_norm_g[layer].reshape(1, -1),
                         (ffn_w_gate[layer].astype(_BF16), ffn_w_up[layer].astype(_BF16)),
                         ffn_dwconv_w[layer], ffn_dwconv_b[layer].reshape(1, -1),
                         ffn_w_down[layer].astype(_BF16), seq=seq)
    return h.reshape(batch, seq, d)
```

```python
import functools
import math

import jax
import jax.numpy as jnp
from jax import lax
from jax.experimental import pallas as pl
from jax.experimental.pallas import tpu as pltpu

D_MODEL = 2048
A_HEADS = 8
A_HALF_DIM = 64
A_V_DIM = 128
B_HEADS = 8
B_NOPE_DIM = 128
B_ROPE_DIM = 64
B_QK_DIM = B_NOPE_DIM + B_ROPE_DIM
B_V_DIM = 128
B_Q_RANK = 512
B_KV_RANK = 256
B_QK_PAD = 256
ROPE_THETA = 10000.0
REL_BUCKETS = 32
REL_MAX_DIST = 128
EPS = 1e-6
A_W = A_HEADS * 2 * A_HALF_DIM
LAT_W = B_Q_RANK + B_KV_RANK + 128

LANES = 128
MXU_DIM = 256
HALO = 16
ATTN_TQ = 256
LOG2E = math.log2(math.e)
VMEM_LIMIT = 56 * 1024 * 1024

_F32 = jnp.float32
_BF16 = jnp.bfloat16


def _rms(xf, g):
    ms = jnp.mean(xf * xf, axis=-1, keepdims=True)
    return xf * lax.rsqrt(ms + EPS) * g


def _dot(a, b):
    return jnp.dot(a, b, preferred_element_type=_F32)


def _dot_nt(a, b):
    return lax.dot_general(a, b, (((1,), (1,)), ((), ())), preferred_element_type=_F32)


def _params(sem):
    return pltpu.CompilerParams(dimension_semantics=sem, vmem_limit_bytes=VMEM_LIMIT)


def _full(a):
    return pl.BlockSpec(a.shape, lambda *_: (0,) * a.ndim)


def _norm_kernel(x_ref, g_ref, o_ref):
    o_ref[...] = _rms(x_ref[...], g_ref[...]).astype(_BF16)


def _norm(x, g, *, tm=1024):
    t = x.shape[0]
    return pl.pallas_call(
        _norm_kernel,
        grid=(t // tm,),
        in_specs=[pl.BlockSpec((tm, D_MODEL), lambda i: (i, 0)), _full(g)],
        out_specs=pl.BlockSpec((tm, D_MODEL), lambda i: (i, 0)),
        out_shape=jax.ShapeDtypeStruct((t, D_MODEL), _BF16),
        compiler_params=_params(("parallel",)),
        name="pre_norm",
    )(x, g)


def _qkv_proj_kernel(h_ref, w_ref, g_ref, bd_ref, o_ref):
    j = pl.program_id(1)
    h = h_ref[...]

    @pl.when(j < 2)
    def _():
        bd = bd_ref[...]
        for c in range(w_ref.shape[1] // MXU_DIM):
            cols = slice(c * MXU_DIM, (c + 1) * MXU_DIM)
            y = _dot(h, w_ref[:, cols])
            ms = _dot((y * y).astype(_BF16), bd) * (1.0 / A_HALF_DIM)
            o_ref[:, cols] = (y * lax.rsqrt(ms + EPS) * g_ref[0, :, cols]).astype(_BF16)

    @pl.when(j == 2)
    def _():
        o_ref[...] = _dot(h, w_ref[...]).astype(_BF16)


def _qkv_proj(h, w, gains, *, tm=512):
    t = h.shape[0]
    grp = jnp.arange(MXU_DIM) // A_HALF_DIM
    bd = (grp[:, None] == grp[None, :]).astype(_BF16)
    return pl.pallas_call(
        _qkv_proj_kernel,
        grid=(t // tm, 3),
        in_specs=[
            pl.BlockSpec((tm, D_MODEL), lambda i, j: (i, 0)),
            pl.BlockSpec((D_MODEL, A_W), lambda i, j: (0, j)),
            pl.BlockSpec((1, 1, A_W), lambda i, j: (jnp.minimum(j, 1), 0, 0)),
            _full(bd),
        ],
        out_specs=pl.BlockSpec((tm, A_W), lambda i, j: (i, j)),
        out_shape=jax.ShapeDtypeStruct((t, 3 * A_W), _BF16),
        compiler_params=_params(("parallel", "arbitrary")),
        name="diff_qkv_proj",
    )(h, w, gains, bd)


def _mla_prep_kernel(h_ref, pos_ref, inv_ref, wl_ref, qag_ref, wuq_ref, kvag_ref, wukv_ref,
                     mqg_ref, mkg_ref, q_ref, k_ref, v_ref):
    lat = _dot(h_ref[...], wl_ref[...])
    cq = lat[:, :B_Q_RANK]
    ckv = lat[:, B_Q_RANK:B_Q_RANK + B_KV_RANK]
    kr = lat[:, B_Q_RANK + B_KV_RANK:]

    ang = pos_ref[...].astype(_F32) * inv_ref[...]
    lane = lax.broadcasted_iota(jnp.int32, (1, LANES), 1)
    half = B_ROPE_DIM // 2
    cosv = jnp.cos(ang)
    sinv = jnp.sin(ang)
    c_tab = jnp.where(lane < B_ROPE_DIM, cosv, 0.0)
    s_lo = jnp.where(lane < half, -sinv, 0.0)
    s_hi = jnp.where((lane >= half) & (lane < B_ROPE_DIM), sinv, 0.0)

    def rope(r):
        return r * c_tab + pltpu.roll(r, LANES - half, 1) * s_lo + pltpu.roll(r, half, 1) * s_hi

    scale = B_QK_DIM ** -0.5 * LOG2E
    qf = _dot(_rms(cq, qag_ref[...]).astype(_BF16), wuq_ref[...])
    mqg = mqg_ref[...]
    for h in range(B_HEADS):
        slab = qf[:, h * B_QK_PAD:(h + 1) * B_QK_PAD]
        ms = jnp.sum(slab * slab, axis=-1, keepdims=True) * (1.0 / B_QK_DIM)
        sn = slab * (lax.rsqrt(ms + EPS) * scale) * mqg
        q_ref[:, h * B_QK_PAD:h * B_QK_PAD + B_NOPE_DIM] = sn[:, :B_NOPE_DIM].astype(_BF16)
        q_ref[:, h * B_QK_PAD + B_NOPE_DIM:(h + 1) * B_QK_PAD] = rope(sn[:, B_NOPE_DIM:]).astype(_BF16)

    kv = _dot(_rms(ckv, kvag_ref[...]).astype(_BF16), wukv_ref[...])
    mkg = mkg_ref[...]
    kr_ss = jnp.sum(kr * kr, axis=-1, keepdims=True)
    kr_rot = rope(kr * mkg[:, B_NOPE_DIM:])
    kv_w = B_NOPE_DIM + B_V_DIM
    for h in range(B_HEADS):
        kn = kv[:, h * kv_w:h * kv_w + B_NOPE_DIM]
        ms = (jnp.sum(kn * kn, axis=-1, keepdims=True) + kr_ss) * (1.0 / B_QK_DIM)
        rs = lax.rsqrt(ms + EPS)
        k_ref[:, h * B_QK_PAD:h * B_QK_PAD + B_NOPE_DIM] = (kn * rs * mkg[:, :B_NOPE_DIM]).astype(_BF16)
        k_ref[:, h * B_QK_PAD + B_NOPE_DIM:(h + 1) * B_QK_PAD] = (kr_rot * rs).astype(_BF16)
        v_ref[:, h * B_V_DIM:(h + 1) * B_V_DIM] = kv[:, h * kv_w + B_NOPE_DIM:(h + 1) * kv_w].astype(_BF16)


def _mla_prep(h, pos, inv, wl, qag, wuq, kvag, wukv, mqg, mkg, *, tm=512):
    t = h.shape[0]
    return pl.pallas_call(
        _mla_prep_kernel,
        grid=(t // tm,),
        in_specs=[
            pl.BlockSpec((tm, D_MODEL), lambda i: (i, 0)),
            pl.BlockSpec((tm, 1), lambda i: (i, 0)),
            _full(inv), _full(wl), _full(qag), _full(wuq), _full(kvag), _full(wukv), _full(mqg), _full(mkg),
        ],
        out_specs=[
            pl.BlockSpec((tm, B_HEADS * B_QK_PAD), lambda i: (i, 0)),
            pl.BlockSpec((tm, B_HEADS * B_QK_PAD), lambda i: (i, 0)),
            pl.BlockSpec((tm, B_HEADS * B_V_DIM), lambda i: (i, 0)),
        ],
        out_shape=[
            jax.ShapeDtypeStruct((t, B_HEADS * B_QK_PAD), _BF16),
            jax.ShapeDtypeStruct((t, B_HEADS * B_QK_PAD), _BF16),
            jax.ShapeDtypeStruct((t, B_HEADS * B_V_DIM), _BF16),
        ],
        compiler_params=_params(("parallel",)),
        name="mla_prep",
    )(h, pos, inv, wl, qag, wuq, kvag, wukv, mqg, mkg)


def _t5_bucket_of(rel):
    nb = REL_BUCKETS // 2
    max_exact = nb // 2
    n = jnp.abs(rel)
    large = jnp.full(rel.shape, max_exact, jnp.int32)
    for k in range(1, nb - max_exact):
        large = large + (n >= math.ceil(max_exact * 2.0 ** (k / 2.0))).astype(jnp.int32)
    return jnp.where(rel > 0, nb, 0) + jnp.where(n < max_exact, n, large)


def _bias_band_kernel(tbl_ref, o_ref, *, seq, tq):
    h = pl.program_id(0)
    blk = REL_MAX_DIST
    row = lax.broadcasted_iota(jnp.int32, (blk, blk), 0)
    col = lax.broadcasted_iota(jnp.int32, (blk, blk), 1)
    tiles = {}
    for d in range(-2, 3):
        rel = d * blk + col - row if abs(d) < 2 else jnp.full((blk, blk), d * blk // 2, jnp.int32)
        bucket = _t5_bucket_of(rel)
        val = jnp.zeros((blk, blk), _F32)
        for c in range(REL_BUCKETS):
            val = jnp.where(bucket == c, tbl_ref[c * A_HEADS + h], val)
        tiles[d] = val * LOG2E
    shift = (seq - tq) // blk
    for rb in range(tq // blk):
        for cb in range((2 * seq - tq) // blk):
            d = max(-2, min(2, cb - rb - shift))
            o_ref[0, rb * blk:(rb + 1) * blk, cb * blk:(cb + 1) * blk] = tiles[d]


def _bias_band(rel_table, *, seq, tq):
    cols = 2 * seq - tq
    return pl.pallas_call(
        functools.partial(_bias_band_kernel, seq=seq, tq=tq),
        grid=(A_HEADS,),
        in_specs=[pl.BlockSpec(memory_space=pltpu.SMEM)],
        out_specs=pl.BlockSpec((1, tq, cols), lambda h: (h, 0, 0)),
        out_shape=jax.ShapeDtypeStruct((A_HEADS, tq, cols), _F32),
        compiler_params=_params(("arbitrary",)),
        name="rel_bias_band",
    )(rel_table.reshape(-1))


def _softmax_pv(s, va):
    m = jnp.max(s, axis=-1, keepdims=True)
    p = jnp.exp2(s - m).astype(_BF16)
    oa = _dot(p, va)
    dv = va.shape[1] // 2
    return oa[:, :dv] / oa[:, dv:]


def _stage_v(v_ref, va_ref):
    dv = v_ref.shape[1]
    va_ref[:, :dv] = v_ref[...]
    va_ref[:, dv:] = jnp.ones(v_ref.shape, _BF16)


def _diff_attn_kernel(q_ref, k_ref, v_ref, band_ref, lq1_ref, lk1_ref, lq2_ref, lk2_ref, sg_ref, o_ref,
                      va_ref, *, tq, seq, lam_init):
    _stage_v(v_ref, va_ref)
    k = k_ref[...]
    va = va_ref[...]
    lane = lax.broadcasted_iota(jnp.int32, (1, LANES), 1)
    lam = (jnp.exp(jnp.sum(lq1_ref[...] * lk1_ref[...], axis=-1, keepdims=True))
           - jnp.exp(jnp.sum(lq2_ref[...] * lk2_ref[...], axis=-1, keepdims=True)) + lam_init)
    sg = sg_ref[...] * (1.0 - lam_init)
    n = seq // tq

    def logits(t):
        q = q_ref[t * tq:(t + 1) * tq]
        zero = jnp.zeros_like(q)
        off = seq - (t + 1) * tq
        bias = band_ref[0, :, off:off + seq]
        return (_dot_nt(jnp.where(lane < A_HALF_DIM, q, zero), k) + bias,
                _dot_nt(jnp.where(lane >= A_HALF_DIM, q, zero), k) + bias)

    s_next = logits(0)
    for t in range(n):
        s1, s2 = s_next
        if t + 1 < n:
            s_next = logits(t + 1)
        o = _softmax_pv(s1, va) - lam * _softmax_pv(s2, va)
        o_ref[t * tq:(t + 1) * tq] = _rms(o, sg).astype(_BF16)


def _diff_attn(qkv, band, lq1, lk1, lq2, lk2, sg, *, batch, seq, lam_init, tq):
    return pl.pallas_call(
        functools.partial(_diff_attn_kernel, tq=tq, seq=seq, lam_init=lam_init),
        grid=(A_HEADS, batch),
        in_specs=[
            pl.BlockSpec((seq, LANES), lambda h, b: (b, h)),
            pl.BlockSpec((seq, LANES), lambda h, b: (b, A_HEADS + h)),
            pl.BlockSpec((seq, LANES), lambda h, b: (b, 2 * A_HEADS + h)),
            pl.BlockSpec((1, tq, band.shape[2]), lambda h, b: (h, 0, 0)),
            _full(lq1), _full(lk1), _full(lq2), _full(lk2), _full(sg),
        ],
        out_specs=pl.BlockSpec((seq, A_V_DIM), lambda h, b: (b, h)),
        out_shape=jax.ShapeDtypeStruct((batch * seq, A_HEADS * A_V_DIM), _BF16),
        scratch_shapes=[pltpu.VMEM((seq, 2 * A_V_DIM), _BF16)],
        compiler_params=_params(("parallel", "arbitrary")),
        name="diff_attn",
    )(qkv, qkv, qkv, band, lq1, lk1, lq2, lk2, sg)


def _mla_attn_kernel(q_ref, k_ref, v_ref, o_ref, va_ref, *, tq, seq):
    _stage_v(v_ref, va_ref)
    k = k_ref[...]
    va = va_ref[...]
    n = seq // tq
    s_next = _dot_nt(q_ref[0:tq], k)
    for t in range(n):
        s = s_next
        if t + 1 < n:
            s_next = _dot_nt(q_ref[(t + 1) * tq:(t + 2) * tq], k)
        o_ref[t * tq:(t + 1) * tq] = _softmax_pv(s, va).astype(_BF16)


def _mla_attn(q, k, v, *, batch, seq, tq):
    return pl.pallas_call(
        functools.partial(_mla_attn_kernel, tq=tq, seq=seq),
        grid=(B_HEADS, batch),
        in_specs=[
            pl.BlockSpec((seq, B_QK_PAD), lambda h, b: (b, h)),
            pl.BlockSpec((seq, B_QK_PAD), lambda h, b: (b, h)),
            pl.BlockSpec((seq, B_V_DIM), lambda h, b: (b, h)),
        ],
        out_specs=pl.BlockSpec((seq, B_V_DIM), lambda h, b: (b, h)),
        out_shape=jax.ShapeDtypeStruct((batch * seq, B_HEADS * B_V_DIM), _BF16),
        scratch_shapes=[pltpu.VMEM((seq, 2 * B_V_DIM), _BF16)],
        compiler_params=_params(("parallel", "arbitrary")),
        name="mla_attn",
    )(q, k, v)


def _out_proj_kernel(x_ref, oa_ref, ob_ref, wa_ref, wb_ref, o_ref):
    o_ref[...] = x_ref[...] + _dot(oa_ref[...], wa_ref[...]) + _dot(ob_ref[...], wb_ref[...])


def _out_proj(x, oa, ob, wa, wb, *, tm=512):
    t = x.shape[0]
    return pl.pallas_call(
        _out_proj_kernel,
        grid=(t // tm,),
        in_specs=[
            pl.BlockSpec((tm, D_MODEL), lambda i: (i, 0)),
            pl.BlockSpec((tm, oa.shape[1]), lambda i: (i, 0)),
            pl.BlockSpec((tm, ob.shape[1]), lambda i: (i, 0)),
            _full(wa), _full(wb),
        ],
        out_specs=pl.BlockSpec((tm, D_MODEL), lambda i: (i, 0)),
        out_shape=jax.ShapeDtypeStruct((t, D_MODEL), _F32),
        compiler_params=_params(("parallel",)),
        name="attn_out_proj",
    )(x, oa, ob, wa, wb)


def _dwconv3(g_ext, cw, tm):
    n = g_ext.shape[0]
    prev = pltpu.roll(g_ext, 1, 0)[HALO:HALO + tm]
    nxt = pltpu.roll(g_ext, n - 1, 0)[HALO:HALO + tm]
    return cw[0:1] * prev + cw[1:2] * g_ext[HALO:HALO + tm] + cw[2:3] * nxt


def _gated_block_kernel(*refs, kind, tm, seq):
    if kind == "ffn":
        x_ref, xp_ref, xn_ref, g_ref, wa_ref, wb_ref, cw_ref, cb_ref, wd_ref, o_ref, hn_ref = refs
    else:
        x_ref, xp_ref, xn_ref, g_ref, wa_ref, wb_ref, wc_ref, cw_ref, wd_ref, o_ref, hn_ref = refs
    i = pl.program_id(0)
    j = pl.program_id(1)
    tiles_per_seq = seq // tm

    @pl.when(j == 0)
    def _():
        g = g_ref[...]
        pos_in_seq = i % tiles_per_seq
        keep_prev = (pos_in_seq != 0).astype(_F32)
        keep_next = (pos_in_seq != tiles_per_seq - 1).astype(_F32)
        hn_ref[0:HALO] = (_rms(xp_ref[...], g) * keep_prev).astype(_BF16)
        hn_ref[HALO:HALO + tm] = _rms(x_ref[...], g).astype(_BF16)
        hn_ref[HALO + tm:] = (_rms(xn_ref[...], g) * keep_next).astype(_BF16)
        o_ref[...] = x_ref[...]

    h_ext = hn_ref[...]
    h_main = hn_ref[HALO:HALO + tm]
    cw = cw_ref[...]
    if kind == "ffn":
        c = _dwconv3(_dot(h_ext, wa_ref[...]), cw, tm) + cb_ref[...]
        act = c * jax.nn.sigmoid(c) * _dot(h_main, wb_ref[...])
    else:
        act = _dot(h_main, wa_ref[...]) * _dwconv3(_dot(h_ext, wb_ref[...]) * _dot(h_ext, wc_ref[...]), cw, tm)
    o_ref[...] += _dot(act.astype(_BF16), wd_ref[...])


def _gated_block(kind, x, g, ups, cw, cb, wd, *, seq, tm=512, tc=512):
    t = x.shape[0]
    c_total = wd.shape[0]
    nc = c_total // tc
    hb = tm // HALO
    row_specs = [
        pl.BlockSpec((tm, D_MODEL), lambda i, j: (i, 0)),
        pl.BlockSpec((HALO, D_MODEL), lambda i, j: (jnp.maximum(i * hb - 1, 0), 0)),
        pl.BlockSpec((HALO, D_MODEL), lambda i, j: (jnp.minimum((i + 1) * hb, t // HALO - 1), 0)),
        pl.BlockSpec((1, D_MODEL), lambda i, j: (0, 0)),
    ]
    if kind == "ffn":
        wg, wu = ups
        args = (x, x, x, g, wg, wu, cw, cb, wd)
        w_specs = [
            pl.BlockSpec((D_MODEL, tc), lambda i, j: (0, j)),
            pl.BlockSpec((D_MODEL, tc), lambda i, j: (0, j)),
            pl.BlockSpec((3, tc), lambda i, j: (0, j)),
            pl.BlockSpec((1, tc), lambda i, j: (0, j)),
        ]
    else:
        (w_in,) = ups
        args = (x, x, x, g, w_in, w_in, w_in, cw, wd)
        w_specs = [
            pl.BlockSpec((D_MODEL, tc), lambda i, j: (0, j)),
            pl.BlockSpec((D_MODEL, tc), lambda i, j: (0, nc + j)),
            pl.BlockSpec((D_MODEL, tc), lambda i, j: (0, 2 * nc + j)),
            pl.BlockSpec((3, tc), lambda i, j: (0, j)),
        ]
    return pl.pallas_call(
        functools.partial(_gated_block_kernel, kind=kind, tm=tm, seq=seq),
        grid=(t // tm, nc),
        in_specs=row_specs + w_specs + [pl.BlockSpec((tc, D_MODEL), lambda i, j: (j, 0))],
        out_specs=pl.BlockSpec((tm, D_MODEL), lambda i, j: (i, 0)),
        out_shape=jax.ShapeDtypeStruct((t, D_MODEL), _F32),
        scratch_shapes=[pltpu.VMEM((tm + 2 * HALO, D_MODEL), _BF16)],
        compiler_params=_params(("parallel", "arbitrary")),
        name=kind + "_block",
    )(*args)


def _pad_lanes(a, width):
    return jnp.pad(a, ((0, 0), (0, width - a.shape[1])))


def _attn_layer(x, pos_col, rel_table, layer_idx, norm_g, w_in, dq_g, dk_g, lq1, lk1, lq2, lk2, subln_g,
                q_a_g, w_uq, kv_a_g, w_ukv, mq_g, mk_g, w_out, *, batch, seq):
    row = lambda a: a.reshape(1, -1)
    lam_init = 0.8 - 0.6 * math.exp(-0.3 * layer_idx)
    hn = _norm(x, row(norm_g))

    reps = A_W // A_HALF_DIM
    gains = jnp.stack([jnp.tile(dq_g, reps) * (A_HALF_DIM ** -0.5 * LOG2E), jnp.tile(dk_g, reps)])
    qkv = _qkv_proj(hn, w_in[:, :3 * A_W].astype(_BF16), gains.reshape(2, 1, A_W))

    w_lat = _pad_lanes(w_in[:, 3 * A_W:], LAT_W).astype(_BF16)
    w_uq_p = jnp.pad(w_uq.reshape(B_Q_RANK, B_HEADS, B_QK_DIM),
                     ((0, 0), (0, 0), (0, B_QK_PAD - B_QK_DIM))).reshape(B_Q_RANK, -1).astype(_BF16)
    inv = 1.0 / (ROPE_THETA ** (jnp.arange(0, B_ROPE_DIM, 2, dtype=_F32) / B_ROPE_DIM))
    inv_lanes = _pad_lanes(jnp.tile(inv, 2).reshape(1, -1), LANES)
    qm, km, vm = _mla_prep(hn, pos_col, inv_lanes, w_lat, row(q_a_g), w_uq_p, row(kv_a_g),
                           w_ukv.astype(_BF16), _pad_lanes(row(mq_g), B_QK_PAD), _pad_lanes(row(mk_g), B_QK_PAD))

    band = _bias_band(rel_table, seq=seq, tq=ATTN_TQ)
    oa = _diff_attn(qkv, band, row(lq1), row(lk1), row(lq2), row(lk2), row(subln_g),
                    batch=batch, seq=seq, lam_init=lam_init, tq=ATTN_TQ)
    ob = _mla_attn(qm, km, vm, batch=batch, seq=seq, tq=ATTN_TQ)
    w_out_b = w_out.astype(_BF16)
    return _out_proj(x, oa, ob, w_out_b[:A_HEADS * A_V_DIM], w_out_b[A_HEADS * A_V_DIM:])


def kernel(x, positions, rel_bias_table, attn_norm_g, attn_w_in, diff_q_norm_g, diff_k_norm_g, diff_lambda_q1, diff_lambda_k1, diff_lambda_q2, diff_lambda_k2, diff_subln_g, mla_q_a_norm_g, mla_w_uq, mla_kv_a_norm_g, mla_w_ukv, mla_q_norm_g, mla_k_norm_g, attn_w_out, conv_norm_g, conv_w_in, conv_w, conv_w_out, ffn_norm_g, ffn_w_gate, ffn_w_up, ffn_dwconv_w, ffn_dwconv_b, ffn_w_down):
    batch, seq, d = x.shape
    depth = ffn_norm_g.shape[0]
    h = x.reshape(batch * seq, d)
    pos_col = positions.reshape(batch * seq, 1)
    for layer in range(depth):
        i = layer // 2
        if layer % 2 == 0:
            h = _attn_layer(h, pos_col, rel_bias_table, layer, attn_norm_g[i], attn_w_in[i], diff_q_norm_g[i],
                            diff_k_norm_g[i], diff_lambda_q1[i], diff_lambda_k1[i], diff_lambda_q2[i],
                            diff_lambda_k2[i], diff_subln_g[i], mla_q_a_norm_g[i], mla_w_uq[i],
                            mla_kv_a_norm_g[i], mla_w_ukv[i], mla_q_norm_g[i], mla_k_norm_g[i], attn_w_out[i],
                            batch=batch, seq=seq)
        else:
            h = _gated_block("conv", h, conv_norm_g[i].reshape(1, -1), (conv_w_in[i].astype(_BF16),),
                             conv_w[i], None, conv_w_out[i].astype(_BF16), seq=seq)
        h = _gated_block("ffn", h, ffn_norm_g[layer].reshape(1, -1),
                         (ffn_w_gate[layer].astype(_BF16), ffn_w_up[layer].astype(_BF16)),
                         ffn_dwconv_w[layer], ffn_dwconv_b[layer].reshape(1, -1),
                         ffn_w_down[layer].astype(_BF16), seq=seq)
    return h.reshape(batch, seq, d)
```

```python
import functools
import math

import jax
import jax.numpy as jnp
from jax import lax
from jax.experimental import pallas as pl
from jax.experimental.pallas import tpu as pltpu

D_MODEL = 2048
A_HEADS = 8
A_HALF_DIM = 64
A_V_DIM = 128
B_HEADS = 8
B_NOPE_DIM = 128
B_ROPE_DIM = 64
B_QK_DIM = B_NOPE_DIM + B_ROPE_DIM
B_V_DIM = 128
B_Q_RANK = 512
B_KV_RANK = 256
B_QK_PAD = 256
ROPE_THETA = 10000.0
REL_BUCKETS = 32
REL_MAX_DIST = 128
EPS = 1e-6
A_W = A_HEADS * 2 * A_HALF_DIM
LAT_W = B_Q_RANK + B_KV_RANK + 128

LANES = 128
MXU_DIM = 256
HALO = 16
ATTN_TQ = 256
GATED_TM = {"ffn": 1024, "conv": 512}
LOG2E = math.log2(math.e)
VMEM_LIMIT = 60 * 1024 * 1024

_F32 = jnp.float32
_BF16 = jnp.bfloat16


def _rms(xf, g):
    ms = jnp.mean(xf * xf, axis=-1, keepdims=True)
    return xf * lax.rsqrt(ms + EPS) * g


def _dot(a, b):
    return jnp.dot(a, b, preferred_element_type=_F32)


def _dot_nt(a, b):
    return lax.dot_general(a, b, (((1,), (1,)), ((), ())), preferred_element_type=_F32)


def _params(sem):
    return pltpu.CompilerParams(dimension_semantics=sem, vmem_limit_bytes=VMEM_LIMIT)


def _full(a):
    return pl.BlockSpec(a.shape, lambda *_: (0,) * a.ndim)


def _cast_kernel(w_ref, o_ref):
    o_ref[...] = w_ref[...].astype(_BF16)


def _to_bf16(w, layer, *, cols=None, tr=256):
    _, r, c = w.shape
    cols = c if cols is None else cols
    return pl.pallas_call(
        _cast_kernel,
        grid=(r // tr,),
        in_specs=[pl.BlockSpec((None, tr, cols), lambda i: (layer, i, 0))],
        out_specs=pl.BlockSpec((tr, cols), lambda i: (i, 0)),
        out_shape=jax.ShapeDtypeStruct((r, cols), _BF16),
        compiler_params=_params(("parallel",)),
        name="weight_to_bf16",
    )(w)


def _norm_kernel(x_ref, g_ref, o_ref):
    o_ref[...] = _rms(x_ref[...], g_ref[...]).astype(_BF16)


def _norm(x, g, *, tm=1024):
    t = x.shape[0]
    return pl.pallas_call(
        _norm_kernel,
        grid=(t // tm,),
        in_specs=[pl.BlockSpec((tm, D_MODEL), lambda i: (i, 0)), _full(g)],
        out_specs=pl.BlockSpec((tm, D_MODEL), lambda i: (i, 0)),
        out_shape=jax.ShapeDtypeStruct((t, D_MODEL), _BF16),
        compiler_params=_params(("parallel",)),
        name="pre_norm",
    )(x, g)


def _qkv_proj_kernel(h_ref, w_ref, g_ref, bd_ref, o_ref):
    j = pl.program_id(1)
    h = h_ref[...]

    @pl.when(j < 2)
    def _():
        bd = bd_ref[...]
        y = _dot(h, w_ref[...])
        y2 = (y * y).astype(_BF16)
        width = bd.shape[0]
        for c in range(w_ref.shape[1] // width):
            cols = slice(c * width, (c + 1) * width)
            ms = _dot(y2[:, cols], bd) * (1.0 / A_HALF_DIM)
            o_ref[:, cols] = (y[:, cols] * lax.rsqrt(ms + EPS) * g_ref[0, :, cols]).astype(_BF16)

    @pl.when(j == 2)
    def _():
        o_ref[...] = _dot(h, w_ref[...]).astype(_BF16)


def _qkv_proj(h, w, gains, *, tm=512):
    t = h.shape[0]
    grp = jnp.arange(2 * MXU_DIM) // A_HALF_DIM
    bd = (grp[:, None] == grp[None, :]).astype(_BF16)
    return pl.pallas_call(
        _qkv_proj_kernel,
        grid=(t // tm, 3),
        in_specs=[
            pl.BlockSpec((tm, D_MODEL), lambda i, j: (i, 0)),
            pl.BlockSpec((D_MODEL, A_W), lambda i, j: (0, j)),
            pl.BlockSpec((1, 1, A_W), lambda i, j: (jnp.minimum(j, 1), 0, 0)),
            _full(bd),
        ],
        out_specs=pl.BlockSpec((tm, A_W), lambda i, j: (i, j)),
        out_shape=jax.ShapeDtypeStruct((t, 3 * A_W), _BF16),
        compiler_params=_params(("parallel", "arbitrary")),
        name="diff_qkv_proj",
    )(h, w, gains, bd)


def _mla_prep_kernel(h_ref, pos_ref, inv_ref, wl_ref, qag_ref, wuq_ref, kvag_ref, wukv_ref,
                     mqg_ref, mkg_ref, q_ref, k_ref, v_ref):
    lat = _dot(h_ref[...], wl_ref[...])
    cq = lat[:, :B_Q_RANK]
    ckv = lat[:, B_Q_RANK:B_Q_RANK + B_KV_RANK]
    kr = lat[:, B_Q_RANK + B_KV_RANK:]

    ang = pos_ref[...].astype(_F32) * inv_ref[...]
    lane = lax.broadcasted_iota(jnp.int32, (1, LANES), 1)
    half = B_ROPE_DIM // 2
    cosv = jnp.cos(ang)
    sinv = jnp.sin(ang)
    c_tab = jnp.where(lane < B_ROPE_DIM, cosv, 0.0)
    s_lo = jnp.where(lane < half, -sinv, 0.0)
    s_hi = jnp.where((lane >= half) & (lane < B_ROPE_DIM), sinv, 0.0)

    def rope(r):
        return r * c_tab + pltpu.roll(r, LANES - half, 1) * s_lo + pltpu.roll(r, half, 1) * s_hi

    scale = B_QK_DIM ** -0.5 * LOG2E
    qf = _dot(_rms(cq, qag_ref[...]).astype(_BF16), wuq_ref[...])
    mqg = mqg_ref[...]
    for h in range(B_HEADS):
        slab = qf[:, h * B_QK_PAD:(h + 1) * B_QK_PAD]
        ms = jnp.sum(slab * slab, axis=-1, keepdims=True) * (1.0 / B_QK_DIM)
        sn = slab * (lax.rsqrt(ms + EPS) * scale) * mqg
        q_ref[:, h * B_QK_PAD:h * B_QK_PAD + B_NOPE_DIM] = sn[:, :B_NOPE_DIM].astype(_BF16)
        q_ref[:, h * B_QK_PAD + B_NOPE_DIM:(h + 1) * B_QK_PAD] = rope(sn[:, B_NOPE_DIM:]).astype(_BF16)

    kv = _dot(_rms(ckv, kvag_ref[...]).astype(_BF16), wukv_ref[...])
    mkg = mkg_ref[...]
    kr_ss = jnp.sum(kr * kr, axis=-1, keepdims=True)
    kr_rot = rope(kr * mkg[:, B_NOPE_DIM:])
    kv_w = B_NOPE_DIM + B_V_DIM
    for h in range(B_HEADS):
        kn = kv[:, h * kv_w:h * kv_w + B_NOPE_DIM]
        ms = (jnp.sum(kn * kn, axis=-1, keepdims=True) + kr_ss) * (1.0 / B_QK_DIM)
        rs = lax.rsqrt(ms + EPS)
        k_ref[:, h * B_QK_PAD:h * B_QK_PAD + B_NOPE_DIM] = (kn * rs * mkg[:, :B_NOPE_DIM]).astype(_BF16)
        k_ref[:, h * B_QK_PAD + B_NOPE_DIM:(h + 1) * B_QK_PAD] = (kr_rot * rs).astype(_BF16)
        v_ref[:, h * B_V_DIM:(h + 1) * B_V_DIM] = kv[:, h * kv_w + B_NOPE_DIM:(h + 1) * kv_w].astype(_BF16)


def _mla_prep(h, pos, inv, wl, qag, wuq, kvag, wukv, mqg, mkg, *, tm=512):
    t = h.shape[0]
    return pl.pallas_call(
        _mla_prep_kernel,
        grid=(t // tm,),
        in_specs=[
            pl.BlockSpec((tm, D_MODEL), lambda i: (i, 0)),
            pl.BlockSpec((tm, 1), lambda i: (i, 0)),
            _full(inv), _full(wl), _full(qag), _full(wuq), _full(kvag), _full(wukv), _full(mqg), _full(mkg),
        ],
        out_specs=[
            pl.BlockSpec((tm, B_HEADS * B_QK_PAD), lambda i: (i, 0)),
            pl.BlockSpec((tm, B_HEADS * B_QK_PAD), lambda i: (i, 0)),
            pl.BlockSpec((tm, B_HEADS * B_V_DIM), lambda i: (i, 0)),
        ],
        out_shape=[
            jax.ShapeDtypeStruct((t, B_HEADS * B_QK_PAD), _BF16),
            jax.ShapeDtypeStruct((t, B_HEADS * B_QK_PAD), _BF16),
            jax.ShapeDtypeStruct((t, B_HEADS * B_V_DIM), _BF16),
        ],
        compiler_params=_params(("parallel",)),
        name="mla_prep",
    )(h, pos, inv, wl, qag, wuq, kvag, wukv, mqg, mkg)


def _t5_bucket_of(rel):
    nb = REL_BUCKETS // 2
    max_exact = nb // 2
    n = jnp.abs(rel)
    large = jnp.full(rel.shape, max_exact, jnp.int32)
    for k in range(1, nb - max_exact):
        large = large + (n >= math.ceil(max_exact * 2.0 ** (k / 2.0))).astype(jnp.int32)
    return jnp.where(rel > 0, nb, 0) + jnp.where(n < max_exact, n, large)


def _bias_band_kernel(tbl_ref, o_ref, *, seq, tq):
    h = pl.program_id(0)
    blk = REL_MAX_DIST
    row = lax.broadcasted_iota(jnp.int32, (blk, blk), 0)
    col = lax.broadcasted_iota(jnp.int32, (blk, blk), 1)
    tiles = {}
    for d in range(-2, 3):
        rel = d * blk + col - row if abs(d) < 2 else jnp.full((blk, blk), d * blk // 2, jnp.int32)
        bucket = _t5_bucket_of(rel)
        val = jnp.zeros((blk, blk), _F32)
        for c in range(REL_BUCKETS):
            val = jnp.where(bucket == c, tbl_ref[c * A_HEADS + h], val)
        tiles[d] = val * LOG2E
    shift = (seq - tq) // blk
    for rb in range(tq // blk):
        for cb in range((2 * seq - tq) // blk):
            d = max(-2, min(2, cb - rb - shift))
            o_ref[0, rb * blk:(rb + 1) * blk, cb * blk:(cb + 1) * blk] = tiles[d]


def _bias_band(rel_table, *, seq, tq):
    cols = 2 * seq - tq
    return pl.pallas_call(
        functools.partial(_bias_band_kernel, seq=seq, tq=tq),
        grid=(A_HEADS,),
        in_specs=[pl.BlockSpec(memory_space=pltpu.SMEM)],
        out_specs=pl.BlockSpec((1, tq, cols), lambda h: (h, 0, 0)),
        out_shape=jax.ShapeDtypeStruct((A_HEADS, tq, cols), _F32),
        compiler_params=_params(("arbitrary",)),
        name="rel_bias_band",
    )(rel_table.reshape(-1))


def _softmax_pv(s, va):
    m = jnp.max(s, axis=-1, keepdims=True)
    p = jnp.exp2(s - m).astype(_BF16)
    oa = _dot(p, va)
    dv = va.shape[1] // 2
    return oa[:, :dv] / oa[:, dv:]


def _stage_v(v_ref, va_ref):
    dv = v_ref.shape[1]
    va_ref[:, :dv] = v_ref[...]
    va_ref[:, dv:] = jnp.ones(v_ref.shape, _BF16)


def _diff_attn_kernel(q_ref, k_ref, v_ref, band_ref, lq1_ref, lk1_ref, lq2_ref, lk2_ref, sg_ref, o_ref,
                      va_ref, *, tq, seq, lam_init):
    _stage_v(v_ref, va_ref)
    k = k_ref[...]
    va = va_ref[...]
    lane = lax.broadcasted_iota(jnp.int32, (1, LANES), 1)
    lam = (jnp.exp(jnp.sum(lq1_ref[...] * lk1_ref[...], axis=-1, keepdims=True))
           - jnp.exp(jnp.sum(lq2_ref[...] * lk2_ref[...], axis=-1, keepdims=True)) + lam_init)
    sg = sg_ref[...] * (1.0 - lam_init)
    n = seq // tq

    def logits(t):
        q = q_ref[t * tq:(t + 1) * tq]
        zero = jnp.zeros_like(q)
        off = seq - (t + 1) * tq
        bias = band_ref[0, :, off:off + seq]
        return (_dot_nt(jnp.where(lane < A_HALF_DIM, q, zero), k) + bias,
                _dot_nt(jnp.where(lane >= A_HALF_DIM, q, zero), k) + bias)

    s_next = logits(0)
    for t in range(n):
        s1, s2 = s_next
        if t + 1 < n:
            s_next = logits(t + 1)
        o = _softmax_pv(s1, va) - lam * _softmax_pv(s2, va)
        o_ref[t * tq:(t + 1) * tq] = _rms(o, sg).astype(_BF16)


def _diff_attn(qkv, band, lq1, lk1, lq2, lk2, sg, *, batch, seq, lam_init, tq):
    return pl.pallas_call(
        functools.partial(_diff_attn_kernel, tq=tq, seq=seq, lam_init=lam_init),
        grid=(A_HEADS, batch),
        in_specs=[
            pl.BlockSpec((seq, LANES), lambda h, b: (b, h)),
            pl.BlockSpec((seq, LANES), lambda h, b: (b, A_HEADS + h)),
            pl.BlockSpec((seq, LANES), lambda h, b: (b, 2 * A_HEADS + h)),
            pl.BlockSpec((1, tq, band.shape[2]), lambda h, b: (h, 0, 0)),
            _full(lq1), _full(lk1), _full(lq2), _full(lk2), _full(sg),
        ],
        out_specs=pl.BlockSpec((seq, A_V_DIM), lambda h, b: (b, h)),
        out_shape=jax.ShapeDtypeStruct((batch * seq, A_HEADS * A_V_DIM), _BF16),
        scratch_shapes=[pltpu.VMEM((seq, 2 * A_V_DIM), _BF16)],
        compiler_params=_params(("parallel", "arbitrary")),
        name="diff_attn",
    )(qkv, qkv, qkv, band, lq1, lk1, lq2, lk2, sg)


def _mla_attn_kernel(q_ref, k_ref, v_ref, o_ref, va_ref, *, tq, seq):
    _stage_v(v_ref, va_ref)
    k = k_ref[...]
    va = va_ref[...]
    n = seq // tq
    s_next = _dot_nt(q_ref[0:tq], k)
    for t in range(n):
        s = s_next
        if t + 1 < n:
            s_next = _dot_nt(q_ref[(t + 1) * tq:(t + 2) * tq], k)
        o_ref[t * tq:(t + 1) * tq] = _softmax_pv(s, va).astype(_BF16)


def _mla_attn(q, k, v, *, batch, seq, tq):
    return pl.pallas_call(
        functools.partial(_mla_attn_kernel, tq=tq, seq=seq),
        grid=(B_HEADS, batch),
        in_specs=[
            pl.BlockSpec((seq, B_QK_PAD), lambda h, b: (b, h)),
            pl.BlockSpec((seq, B_QK_PAD), lambda h, b: (b, h)),
            pl.BlockSpec((seq, B_V_DIM), lambda h, b: (b, h)),
        ],
        out_specs=pl.BlockSpec((seq, B_V_DIM), lambda h, b: (b, h)),
        out_shape=jax.ShapeDtypeStruct((batch * seq, B_HEADS * B_V_DIM), _BF16),
        scratch_shapes=[pltpu.VMEM((seq, 2 * B_V_DIM), _BF16)],
        compiler_params=_params(("parallel", "arbitrary")),
        name="mla_attn",
    )(q, k, v)


def _out_proj_kernel(x_ref, oa_ref, ob_ref, wa_ref, wb_ref, o_ref):
    o_ref[...] = x_ref[...] + _dot(oa_ref[...], wa_ref[...]) + _dot(ob_ref[...], wb_ref[...])


def _out_proj(x, oa, ob, w, *, tm=512):
    t = x.shape[0]
    ka, kb = oa.shape[1], ob.shape[1]
    assert ka == kb and ka + kb == w.shape[0]
    return pl.pallas_call(
        _out_proj_kernel,
        grid=(t // tm,),
        in_specs=[
            pl.BlockSpec((tm, D_MODEL), lambda i: (i, 0)),
            pl.BlockSpec((tm, ka), lambda i: (i, 0)),
            pl.BlockSpec((tm, kb), lambda i: (i, 0)),
            pl.BlockSpec((ka, D_MODEL), lambda i: (0, 0)),
            pl.BlockSpec((kb, D_MODEL), lambda i: (1, 0)),
        ],
        out_specs=pl.BlockSpec((tm, D_MODEL), lambda i: (i, 0)),
        out_shape=jax.ShapeDtypeStruct((t, D_MODEL), _F32),
        compiler_params=_params(("parallel",)),
        name="attn_out_proj",
    )(x, oa, ob, w, w)


def _dwconv3(g_ext, cw, tm):
    n = g_ext.shape[0]
    prev = pltpu.roll(g_ext, 1, 0)[HALO:HALO + tm]
    nxt = pltpu.roll(g_ext, n - 1, 0)[HALO:HALO + tm]
    return cw[0:1] * prev + cw[1:2] * g_ext[HALO:HALO + tm] + cw[2:3] * nxt


def _gated_block_kernel(*refs, kind, tm, seq):
    if kind == "ffn":
        x_ref, xp_ref, xn_ref, g_ref, wa_ref, wb_ref, cw_ref, cb_ref, wd_ref, o_ref, hn_ref = refs
    else:
        x_ref, xp_ref, xn_ref, g_ref, wa_ref, wb_ref, wc_ref, cw_ref, wd_ref, o_ref, hn_ref = refs
    i = pl.program_id(0)
    j = pl.program_id(1)
    tiles_per_seq = seq // tm

    @pl.when(j == 0)
    def _():
        g = g_ref[...]
        pos_in_seq = i % tiles_per_seq
        keep_prev = (pos_in_seq != 0).astype(_F32)
        keep_next = (pos_in_seq != tiles_per_seq - 1).astype(_F32)
        hn_ref[0:HALO] = (_rms(xp_ref[...], g) * keep_prev).astype(_BF16)
        hn_ref[HALO:HALO + tm] = _rms(x_ref[...], g).astype(_BF16)
        hn_ref[HALO + tm:] = (_rms(xn_ref[...], g) * keep_next).astype(_BF16)
        o_ref[...] = x_ref[...]

    h_ext = hn_ref[...]
    h_main = hn_ref[HALO:HALO + tm]
    cw = cw_ref[...]
    if kind == "ffn":
        c = _dwconv3(_dot(h_ext, wa_ref[...]), cw, tm) + cb_ref[...]
        act = c * jax.nn.sigmoid(c) * _dot(h_main, wb_ref[...])
    else:
        act = _dot(h_main, wa_ref[...]) * _dwconv3(_dot(h_ext, wb_ref[...]) * _dot(h_ext, wc_ref[...]), cw, tm)
    o_ref[...] += _dot(act.astype(_BF16), wd_ref[...])


def _gated_block(kind, x, g, ups, cw, cb, wd, *, seq, tc=512):
    t = x.shape[0]
    tm = GATED_TM[kind]
    c_total = wd.shape[0]
    nc = c_total // tc
    hb = tm // HALO
    row_specs = [
        pl.BlockSpec((tm, D_MODEL), lambda i, j: (i, 0), pipeline_mode=pl.Buffered(1)),
        pl.BlockSpec((HALO, D_MODEL), lambda i, j: (jnp.maximum(i * hb - 1, 0), 0)),
        pl.BlockSpec((HALO, D_MODEL), lambda i, j: (jnp.minimum((i + 1) * hb, t // HALO - 1), 0)),
        pl.BlockSpec((1, D_MODEL), lambda i, j: (0, 0)),
    ]
    if kind == "ffn":
        wg, wu = ups
        args = (x, x, x, g, wg, wu, cw, cb, wd)
        w_specs = [
            pl.BlockSpec((D_MODEL, tc), lambda i, j: (0, j)),
            pl.BlockSpec((D_MODEL, tc), lambda i, j: (0, j)),
            pl.BlockSpec((3, tc), lambda i, j: (0, j)),
            pl.BlockSpec((1, tc), lambda i, j: (0, j)),
        ]
    else:
        (w_in,) = ups
        args = (x, x, x, g, w_in, w_in, w_in, cw, wd)
        w_specs = [
            pl.BlockSpec((D_MODEL, tc), lambda i, j: (0, j)),
            pl.BlockSpec((D_MODEL, tc), lambda i, j: (0, nc + j)),
            pl.BlockSpec((D_MODEL, tc), lambda i, j: (0, 2 * nc + j)),
            pl.BlockSpec((3, tc), lambda i, j: (0, j)),
        ]
    return pl.pallas_call(
        functools.partial(_gated_block_kernel, kind=kind, tm=tm, seq=seq),
        grid=(t // tm, nc),
        in_specs=row_specs + w_specs + [pl.BlockSpec((tc, D_MODEL), lambda i, j: (j, 0))],
        out_specs=pl.BlockSpec((tm, D_MODEL), lambda i, j: (i, 0)),
        out_shape=jax.ShapeDtypeStruct((t, D_MODEL), _F32),
        scratch_shapes=[pltpu.VMEM((tm + 2 * HALO, D_MODEL), _BF16)],
        compiler_params=_params(("parallel", "arbitrary")),
        name=kind + "_block",
    )(*args)


def _pad_lanes(a, width):
    return jnp.pad(a, ((0, 0), (0, width - a.shape[1])))


def _attn_layer(x, pos_col, rel_table, layer_idx, norm_g, w_in, w_in_a, dq_g, dk_g, lq1, lk1, lq2, lk2, subln_g,
                q_a_g, w_uq, kv_a_g, w_ukv, mq_g, mk_g, w_out, *, batch, seq):
    row = lambda a: a.reshape(1, -1)
    lam_init = 0.8 - 0.6 * math.exp(-0.3 * layer_idx)
    hn = _norm(x, row(norm_g))

    reps = A_W // A_HALF_DIM
    gains = jnp.stack([jnp.tile(dq_g, reps) * (A_HALF_DIM ** -0.5 * LOG2E), jnp.tile(dk_g, reps)])
    qkv = _qkv_proj(hn, w_in_a, gains.reshape(2, 1, A_W))

    w_lat = _pad_lanes(w_in[:, 3 * A_W:], LAT_W).astype(_BF16)
    w_uq_p = jnp.pad(w_uq.reshape(B_Q_RANK, B_HEADS, B_QK_DIM),
                     ((0, 0), (0, 0), (0, B_QK_PAD - B_QK_DIM))).reshape(B_Q_RANK, -1).astype(_BF16)
    inv = 1.0 / (ROPE_THETA ** (jnp.arange(0, B_ROPE_DIM, 2, dtype=_F32) / B_ROPE_DIM))
    inv_lanes = _pad_lanes(jnp.tile(inv, 2).reshape(1, -1), LANES)
    qm, km, vm = _mla_prep(hn, pos_col, inv_lanes, w_lat, row(q_a_g), w_uq_p, row(kv_a_g),
                           w_ukv.astype(_BF16), _pad_lanes(row(mq_g), B_QK_PAD), _pad_lanes(row(mk_g), B_QK_PAD))

    band = _bias_band(rel_table, seq=seq, tq=ATTN_TQ)
    oa = _diff_attn(qkv, band, row(lq1), row(lk1), row(lq2), row(lk2), row(subln_g),
                    batch=batch, seq=seq, lam_init=lam_init, tq=ATTN_TQ)
    ob = _mla_attn(qm, km, vm, batch=batch, seq=seq, tq=ATTN_TQ)
    return _out_proj(x, oa, ob, w_out)


def kernel(x, positions, rel_bias_table, attn_norm_g, attn_w_in, diff_q_norm_g, diff_k_norm_g, diff_lambda_q1, diff_lambda_k1, diff_lambda_q2, diff_lambda_k2, diff_subln_g, mla_q_a_norm_g, mla_w_uq, mla_kv_a_norm_g, mla_w_ukv, mla_q_norm_g, mla_k_norm_g, attn_w_out, conv_norm_g, conv_w_in, conv_w, conv_w_out, ffn_norm_g, ffn_w_gate, ffn_w_up, ffn_dwconv_w, ffn_dwconv_b, ffn_w_down):
    batch, seq, d = x.shape
    depth = ffn_norm_g.shape[0]
    h = x.reshape(batch * seq, d)
    pos_col = positions.reshape(batch * seq, 1)
    for layer in range(depth):
        i = layer // 2
        if layer % 2 == 0:
            h = _attn_layer(h, pos_col, rel_bias_table, layer, attn_norm_g[i], attn_w_in[i],
                            _to_bf16(attn_w_in, i, cols=3 * A_W), diff_q_norm_g[i],
                            diff_k_norm_g[i], diff_lambda_q1[i], diff_lambda_k1[i], diff_lambda_q2[i],
                            diff_lambda_k2[i], diff_subln_g[i], mla_q_a_norm_g[i], mla_w_uq[i],
                            mla_kv_a_norm_g[i], mla_w_ukv[i], mla_q_norm_g[i], mla_k_norm_g[i],
                            _to_bf16(attn_w_out, i), batch=batch, seq=seq)
        else:
            h = _gated_block("conv", h, conv_norm_g[i].reshape(1, -1), (_to_bf16(conv_w_in, i),),
                             conv_w[i], None, _to_bf16(conv_w_out, i), seq=seq)
        h = _gated_block("ffn", h, ffn_norm_g[layer].reshape(1, -1),
                         (_to_bf16(ffn_w_gate, layer), _to_bf16(ffn_w_up, layer)),
                         ffn_dwconv_w[layer], ffn_dwconv_b[layer].reshape(1, -1),
                         _to_bf16(ffn_w_down, layer), seq=seq)
    return h.reshape(batch, seq, d)
```

```python
import functools
import math

import jax
import jax.numpy as jnp
from jax import lax
from jax.experimental import pallas as pl
from jax.experimental.pallas import tpu as pltpu

D_MODEL = 2048
A_HEADS = 8
A_HALF_DIM = 64
A_V_DIM = 128
B_HEADS = 8
B_NOPE_DIM = 128
B_ROPE_DIM = 64
B_QK_DIM = B_NOPE_DIM + B_ROPE_DIM
B_V_DIM = 128
B_Q_RANK = 512
B_KV_RANK = 256
B_QK_PAD = 256
ROPE_THETA = 10000.0
REL_BUCKETS = 32
REL_MAX_DIST = 128
EPS = 1e-6
A_W = A_HEADS * 2 * A_HALF_DIM
LAT_W = B_Q_RANK + B_KV_RANK + 128

LANES = 128
MXU_DIM = 256
HALO = 16
ATTN_TQ = 256
GATED_TM = {"ffn": 1024, "conv": 512}
GATED_SUB = 512
LOG2E = math.log2(math.e)
VMEM_LIMIT = 62 * 1024 * 1024

_F32 = jnp.float32
_BF16 = jnp.bfloat16


def _rms(xf, g):
    ms = jnp.mean(xf * xf, axis=-1, keepdims=True)
    return xf * lax.rsqrt(ms + EPS) * g


def _dot(a, b):
    return jnp.dot(a, b, preferred_element_type=_F32)


def _dot_nt(a, b):
    return lax.dot_general(a, b, (((1,), (1,)), ((), ())), preferred_element_type=_F32)


def _params(sem):
    return pltpu.CompilerParams(dimension_semantics=sem, vmem_limit_bytes=VMEM_LIMIT)


def _full(a):
    return pl.BlockSpec(a.shape, lambda *_: (0,) * a.ndim)


def _cast_kernel(w_ref, o_ref):
    o_ref[...] = w_ref[...].astype(_BF16)


def _to_bf16(w, layer, *, cols=None, tr=256):
    _, r, c = w.shape
    cols = c if cols is None else cols
    return pl.pallas_call(
        _cast_kernel,
        grid=(r // tr,),
        in_specs=[pl.BlockSpec((None, tr, cols), lambda i: (layer, i, 0))],
        out_specs=pl.BlockSpec((tr, cols), lambda i: (i, 0)),
        out_shape=jax.ShapeDtypeStruct((r, cols), _BF16),
        compiler_params=_params(("parallel",)),
        name="weight_to_bf16",
    )(w)


def _norm_kernel(x_ref, g_ref, o_ref):
    o_ref[...] = _rms(x_ref[...], g_ref[...]).astype(_BF16)


def _norm(x, g, *, tm=1024):
    t = x.shape[0]
    return pl.pallas_call(
        _norm_kernel,
        grid=(t // tm,),
        in_specs=[pl.BlockSpec((tm, D_MODEL), lambda i: (i, 0)), _full(g)],
        out_specs=pl.BlockSpec((tm, D_MODEL), lambda i: (i, 0)),
        out_shape=jax.ShapeDtypeStruct((t, D_MODEL), _BF16),
        compiler_params=_params(("parallel",)),
        name="pre_norm",
    )(x, g)


def _qkv_proj_kernel(h_ref, w_ref, g_ref, bd_ref, o_ref):
    j = pl.program_id(0)
    h = h_ref[...]

    @pl.when(j < 2)
    def _():
        bd = bd_ref[...]
        y = _dot(h, w_ref[...])
        y2 = (y * y).astype(_BF16)
        width = bd.shape[0]
        for c in range(w_ref.shape[1] // width):
            cols = slice(c * width, (c + 1) * width)
            ms = _dot(y2[:, cols], bd) * (1.0 / A_HALF_DIM)
            o_ref[:, cols] = (y[:, cols] * lax.rsqrt(ms + EPS) * g_ref[0, :, cols]).astype(_BF16)

    @pl.when(j == 2)
    def _():
        o_ref[...] = _dot(h, w_ref[...]).astype(_BF16)


def _qkv_proj(h, w, gains, *, tm=512):
    t = h.shape[0]
    grp = jnp.arange(2 * MXU_DIM) // A_HALF_DIM
    bd = (grp[:, None] == grp[None, :]).astype(_BF16)
    return pl.pallas_call(
        _qkv_proj_kernel,
        grid=(3, t // tm),
        in_specs=[
            pl.BlockSpec((tm, D_MODEL), lambda j, i: (i, 0)),
            pl.BlockSpec((D_MODEL, A_W), lambda j, i: (0, j)),
            pl.BlockSpec((1, 1, A_W), lambda j, i: (jnp.minimum(j, 1), 0, 0)),
            _full(bd),
        ],
        out_specs=pl.BlockSpec((tm, A_W), lambda j, i: (i, j)),
        out_shape=jax.ShapeDtypeStruct((t, 3 * A_W), _BF16),
        compiler_params=_params(("arbitrary", "arbitrary")),
        name="diff_qkv_proj",
    )(h, w, gains, bd)


def _mla_prep_kernel(h_ref, pos_ref, inv_ref, wl_ref, qag_ref, wuq_ref, kvag_ref, wukv_ref,
                     mqg_ref, mkg_ref, q_ref, k_ref, v_ref):
    lat = _dot(h_ref[...], wl_ref[...])
    cq = lat[:, :B_Q_RANK]
    ckv = lat[:, B_Q_RANK:B_Q_RANK + B_KV_RANK]
    kr = lat[:, B_Q_RANK + B_KV_RANK:]

    ang = pos_ref[...].astype(_F32) * inv_ref[...]
    lane = lax.broadcasted_iota(jnp.int32, (1, LANES), 1)
    half = B_ROPE_DIM // 2
    cosv = jnp.cos(ang)
    sinv = jnp.sin(ang)
    c_tab = jnp.where(lane < B_ROPE_DIM, cosv, 0.0)
    s_lo = jnp.where(lane < half, -sinv, 0.0)
    s_hi = jnp.where((lane >= half) & (lane < B_ROPE_DIM), sinv, 0.0)

    def rope(r):
        return r * c_tab + pltpu.roll(r, LANES - half, 1) * s_lo + pltpu.roll(r, half, 1) * s_hi

    scale = B_QK_DIM ** -0.5 * LOG2E
    qf = _dot(_rms(cq, qag_ref[...]).astype(_BF16), wuq_ref[...])
    mqg = mqg_ref[...]
    for h in range(B_HEADS):
        slab = qf[:, h * B_QK_PAD:(h + 1) * B_QK_PAD]
        ms = jnp.sum(slab * slab, axis=-1, keepdims=True) * (1.0 / B_QK_DIM)
        sn = slab * (lax.rsqrt(ms + EPS) * scale) * mqg
        q_ref[:, h * B_QK_PAD:h * B_QK_PAD + B_NOPE_DIM] = sn[:, :B_NOPE_DIM].astype(_BF16)
        q_ref[:, h * B_QK_PAD + B_NOPE_DIM:(h + 1) * B_QK_PAD] = rope(sn[:, B_NOPE_DIM:]).astype(_BF16)

    kv = _dot(_rms(ckv, kvag_ref[...]).astype(_BF16), wukv_ref[...])
    mkg = mkg_ref[...]
    kr_ss = jnp.sum(kr * kr, axis=-1, keepdims=True)
    kr_rot = rope(kr * mkg[:, B_NOPE_DIM:])
    kv_w = B_NOPE_DIM + B_V_DIM
    for h in range(B_HEADS):
        kn = kv[:, h * kv_w:h * kv_w + B_NOPE_DIM]
        ms = (jnp.sum(kn * kn, axis=-1, keepdims=True) + kr_ss) * (1.0 / B_QK_DIM)
        rs = lax.rsqrt(ms + EPS)
        k_ref[:, h * B_QK_PAD:h * B_QK_PAD + B_NOPE_DIM] = (kn * rs * mkg[:, :B_NOPE_DIM]).astype(_BF16)
        k_ref[:, h * B_QK_PAD + B_NOPE_DIM:(h + 1) * B_QK_PAD] = (kr_rot * rs).astype(_BF16)
        v_ref[:, h * B_V_DIM:(h + 1) * B_V_DIM] = kv[:, h * kv_w + B_NOPE_DIM:(h + 1) * kv_w].astype(_BF16)


def _mla_prep(h, pos, inv, wl, qag, wuq, kvag, wukv, mqg, mkg, *, tm=512):
    t = h.shape[0]
    return pl.pallas_call(
        _mla_prep_kernel,
        grid=(t // tm,),
        in_specs=[
            pl.BlockSpec((tm, D_MODEL), lambda i: (i, 0)),
            pl.BlockSpec((tm, 1), lambda i: (i, 0)),
            _full(inv), _full(wl), _full(qag), _full(wuq), _full(kvag), _full(wukv), _full(mqg), _full(mkg),
        ],
        out_specs=[
            pl.BlockSpec((tm, B_HEADS * B_QK_PAD), lambda i: (i, 0)),
            pl.BlockSpec((tm, B_HEADS * B_QK_PAD), lambda i: (i, 0)),
            pl.BlockSpec((tm, B_HEADS * B_V_DIM), lambda i: (i, 0)),
        ],
        out_shape=[
            jax.ShapeDtypeStruct((t, B_HEADS * B_QK_PAD), _BF16),
            jax.ShapeDtypeStruct((t, B_HEADS * B_QK_PAD), _BF16),
            jax.ShapeDtypeStruct((t, B_HEADS * B_V_DIM), _BF16),
        ],
        compiler_params=_params(("parallel",)),
        name="mla_prep",
    )(h, pos, inv, wl, qag, wuq, kvag, wukv, mqg, mkg)


def _t5_bucket_of(rel):
    nb = REL_BUCKETS // 2
    max_exact = nb // 2
    n = jnp.abs(rel)
    large = jnp.full(rel.shape, max_exact, jnp.int32)
    for k in range(1, nb - max_exact):
        large = large + (n >= math.ceil(max_exact * 2.0 ** (k / 2.0))).astype(jnp.int32)
    return jnp.where(rel > 0, nb, 0) + jnp.where(n < max_exact, n, large)


def _bias_band_kernel(tbl_ref, o_ref, *, seq, tq):
    h = pl.program_id(0)
    blk = REL_MAX_DIST
    row = lax.broadcasted_iota(jnp.int32, (blk, blk), 0)
    col = lax.broadcasted_iota(jnp.int32, (blk, blk), 1)
    tiles = {}
    for d in range(-2, 3):
        rel = d * blk + col - row if abs(d) < 2 else jnp.full((blk, blk), d * blk // 2, jnp.int32)
        bucket = _t5_bucket_of(rel)
        val = jnp.zeros((blk, blk), _F32)
        for c in range(REL_BUCKETS):
            val = jnp.where(bucket == c, tbl_ref[c * A_HEADS + h], val)
        tiles[d] = val * LOG2E
    shift = (seq - tq) // blk
    for rb in range(tq // blk):
        for cb in range((2 * seq - tq) // blk):
            d = max(-2, min(2, cb - rb - shift))
            o_ref[0, rb * blk:(rb + 1) * blk, cb * blk:(cb + 1) * blk] = tiles[d]


def _bias_band(rel_table, *, seq, tq):
    cols = 2 * seq - tq
    return pl.pallas_call(
        functools.partial(_bias_band_kernel, seq=seq, tq=tq),
        grid=(A_HEADS,),
        in_specs=[pl.BlockSpec(memory_space=pltpu.SMEM)],
        out_specs=pl.BlockSpec((1, tq, cols), lambda h: (h, 0, 0)),
        out_shape=jax.ShapeDtypeStruct((A_HEADS, tq, cols), _F32),
        compiler_params=_params(("arbitrary",)),
        name="rel_bias_band",
    )(rel_table.reshape(-1))


def _softmax_pv(s, va):
    m = jnp.max(s, axis=-1, keepdims=True)
    p = jnp.exp2(s - m).astype(_BF16)
    oa = _dot(p, va)
    dv = va.shape[1] // 2
    return oa[:, :dv] / oa[:, dv:]


def _stage_v(v_ref, va_ref):
    dv = v_ref.shape[1]
    va_ref[:, :dv] = v_ref[...]
    va_ref[:, dv:] = jnp.ones(v_ref.shape, _BF16)


def _diff_attn_kernel(q_ref, k_ref, v_ref, band_ref, lq1_ref, lk1_ref, lq2_ref, lk2_ref, sg_ref, o_ref,
                      va_ref, *, tq, seq, lam_init):
    _stage_v(v_ref, va_ref)
    k = k_ref[...]
    va = va_ref[...]
    lane = lax.broadcasted_iota(jnp.int32, (1, LANES), 1)
    lam = (jnp.exp(jnp.sum(lq1_ref[...] * lk1_ref[...], axis=-1, keepdims=True))
           - jnp.exp(jnp.sum(lq2_ref[...] * lk2_ref[...], axis=-1, keepdims=True)) + lam_init)
    sg = sg_ref[...] * (1.0 - lam_init)
    n = seq // tq

    def logits(t):
        q = q_ref[t * tq:(t + 1) * tq]
        zero = jnp.zeros_like(q)
        off = seq - (t + 1) * tq
        bias = band_ref[0, :, off:off + seq]
        return (_dot_nt(jnp.where(lane < A_HALF_DIM, q, zero), k) + bias,
                _dot_nt(jnp.where(lane >= A_HALF_DIM, q, zero), k) + bias)

    s_next = logits(0)
    for t in range(n):
        s1, s2 = s_next
        if t + 1 < n:
            s_next = logits(t + 1)
        o = _softmax_pv(s1, va) - lam * _softmax_pv(s2, va)
        o_ref[t * tq:(t + 1) * tq] = _rms(o, sg).astype(_BF16)


def _diff_attn(qkv, band, lq1, lk1, lq2, lk2, sg, *, batch, seq, lam_init, tq):
    return pl.pallas_call(
        functools.partial(_diff_attn_kernel, tq=tq, seq=seq, lam_init=lam_init),
        grid=(A_HEADS, batch),
        in_specs=[
            pl.BlockSpec((seq, LANES), lambda h, b: (b, h)),
            pl.BlockSpec((seq, LANES), lambda h, b: (b, A_HEADS + h)),
            pl.BlockSpec((seq, LANES), lambda h, b: (b, 2 * A_HEADS + h)),
            pl.BlockSpec((1, tq, band.shape[2]), lambda h, b: (h, 0, 0)),
            _full(lq1), _full(lk1), _full(lq2), _full(lk2), _full(sg),
        ],
        out_specs=pl.BlockSpec((seq, A_V_DIM), lambda h, b: (b, h)),
        out_shape=jax.ShapeDtypeStruct((batch * seq, A_HEADS * A_V_DIM), _BF16),
        scratch_shapes=[pltpu.VMEM((seq, 2 * A_V_DIM), _BF16)],
        compiler_params=_params(("parallel", "arbitrary")),
        name="diff_attn",
    )(qkv, qkv, qkv, band, lq1, lk1, lq2, lk2, sg)


def _mla_attn_kernel(q_ref, k_ref, v_ref, o_ref, va_ref, *, tq, seq):
    _stage_v(v_ref, va_ref)
    k = k_ref[...]
    va = va_ref[...]
    n = seq // tq
    s_next = _dot_nt(q_ref[0:tq], k)
    for t in range(n):
        s = s_next
        if t + 1 < n:
            s_next = _dot_nt(q_ref[(t + 1) * tq:(t + 2) * tq], k)
        o_ref[t * tq:(t + 1) * tq] = _softmax_pv(s, va).astype(_BF16)


def _mla_attn(q, k, v, *, batch, seq, tq):
    return pl.pallas_call(
        functools.partial(_mla_attn_kernel, tq=tq, seq=seq),
        grid=(B_HEADS, batch),
        in_specs=[
            pl.BlockSpec((seq, B_QK_PAD), lambda h, b: (b, h)),
            pl.BlockSpec((seq, B_QK_PAD), lambda h, b: (b, h)),
            pl.BlockSpec((seq, B_V_DIM), lambda h, b: (b, h)),
        ],
        out_specs=pl.BlockSpec((seq, B_V_DIM), lambda h, b: (b, h)),
        out_shape=jax.ShapeDtypeStruct((batch * seq, B_HEADS * B_V_DIM), _BF16),
        scratch_shapes=[pltpu.VMEM((seq, 2 * B_V_DIM), _BF16)],
        compiler_params=_params(("parallel", "arbitrary")),
        name="mla_attn",
    )(q, k, v)


def _out_proj_kernel(x_ref, oa_ref, ob_ref, wa_ref, wb_ref, o_ref):
    o_ref[...] = x_ref[...] + _dot(oa_ref[...], wa_ref[...]) + _dot(ob_ref[...], wb_ref[...])


def _out_proj(x, oa, ob, w, *, tm=512):
    t = x.shape[0]
    ka, kb = oa.shape[1], ob.shape[1]
    assert ka == kb and ka + kb == w.shape[0]
    return pl.pallas_call(
        _out_proj_kernel,
        grid=(t // tm,),
        in_specs=[
            pl.BlockSpec((tm, D_MODEL), lambda i: (i, 0)),
            pl.BlockSpec((tm, ka), lambda i: (i, 0)),
            pl.BlockSpec((tm, kb), lambda i: (i, 0)),
            pl.BlockSpec((ka, D_MODEL), lambda i: (0, 0)),
            pl.BlockSpec((kb, D_MODEL), lambda i: (1, 0)),
        ],
        out_specs=pl.BlockSpec((tm, D_MODEL), lambda i: (i, 0)),
        out_shape=jax.ShapeDtypeStruct((t, D_MODEL), _F32),
        compiler_params=_params(("parallel",)),
        name="attn_out_proj",
    )(x, oa, ob, w, w)


def _dwconv3(g_ext, cw, tm):
    n = g_ext.shape[0]
    prev = pltpu.roll(g_ext, 1, 0)[HALO:HALO + tm]
    nxt = pltpu.roll(g_ext, n - 1, 0)[HALO:HALO + tm]
    return cw[0:1] * prev + cw[1:2] * g_ext[HALO:HALO + tm] + cw[2:3] * nxt


def _gated_block_kernel(*refs, kind, tm, seq):
    if kind == "ffn":
        x_ref, xp_ref, xn_ref, g_ref, wa_ref, wb_ref, cw_ref, cb_ref, wd_ref, o_ref, hn_ref = refs
    else:
        x_ref, xp_ref, xn_ref, g_ref, wa_ref, wb_ref, wc_ref, cw_ref, wd_ref, o_ref, hn_ref = refs
    i = pl.program_id(0)
    j = pl.program_id(1)
    tiles_per_seq = seq // tm

    @pl.when(j == 0)
    def _():
        g = g_ref[...]
        pos_in_seq = i % tiles_per_seq
        keep_prev = (pos_in_seq != 0).astype(_F32)
        keep_next = (pos_in_seq != tiles_per_seq - 1).astype(_F32)
        hn_ref[0:HALO] = (_rms(xp_ref[...], g) * keep_prev).astype(_BF16)
        hn_ref[HALO:HALO + tm] = _rms(x_ref[...], g).astype(_BF16)
        hn_ref[HALO + tm:] = (_rms(xn_ref[...], g) * keep_next).astype(_BF16)
        o_ref[...] = x_ref[...]

    cw = cw_ref[...]
    sub = min(tm, GATED_SUB)
    for lo in range(0, tm, sub):
        h_ext = hn_ref[lo:lo + sub + 2 * HALO]
        h_main = hn_ref[lo + HALO:lo + HALO + sub]
        if kind == "ffn":
            c = _dwconv3(_dot(h_ext, wa_ref[...]), cw, sub) + cb_ref[...]
            act = c * jax.nn.sigmoid(c) * _dot(h_main, wb_ref[...])
        else:
            act = _dot(h_main, wa_ref[...]) * _dwconv3(_dot(h_ext, wb_ref[...]) * _dot(h_ext, wc_ref[...]), cw, sub)
        o_ref[lo:lo + sub] += _dot(act.astype(_BF16), wd_ref[...])


def _gated_block(kind, x, g, ups, cw, cb, wd, *, seq, tc=512):
    t = x.shape[0]
    tm = GATED_TM[kind]
    c_total = wd.shape[0]
    nc = c_total // tc
    hb = tm // HALO
    row_specs = [
        pl.BlockSpec((tm, D_MODEL), lambda i, j: (i, 0)),
        pl.BlockSpec((HALO, D_MODEL), lambda i, j: (jnp.maximum(i * hb - 1, 0), 0)),
        pl.BlockSpec((HALO, D_MODEL), lambda i, j: (jnp.minimum((i + 1) * hb, t // HALO - 1), 0)),
        pl.BlockSpec((1, D_MODEL), lambda i, j: (0, 0)),
    ]
    if kind == "ffn":
        wg, wu = ups
        args = (x, x, x, g, wg, wu, cw, cb, wd)
        w_specs = [
            pl.BlockSpec((D_MODEL, tc), lambda i, j: (0, j)),
            pl.BlockSpec((D_MODEL, tc), lambda i, j: (0, j)),
            pl.BlockSpec((3, tc), lambda i, j: (0, j)),
            pl.BlockSpec((1, tc), lambda i, j: (0, j)),
        ]
    else:
        (w_in,) = ups
        args = (x, x, x, g, w_in, w_in, w_in, cw, wd)
        w_specs = [
            pl.BlockSpec((D_MODEL, tc), lambda i, j: (0, j)),
            pl.BlockSpec((D_MODEL, tc), lambda i, j: (0, nc + j)),
            pl.BlockSpec((D_MODEL, tc), lambda i, j: (0, 2 * nc + j)),
            pl.BlockSpec((3, tc), lambda i, j: (0, j)),
        ]
    return pl.pallas_call(
        functools.partial(_gated_block_kernel, kind=kind, tm=tm, seq=seq),
        grid=(t // tm, nc),
        in_specs=row_specs + w_specs + [pl.BlockSpec((tc, D_MODEL), lambda i, j: (j, 0))],
        out_specs=pl.BlockSpec((tm, D_MODEL), lambda i, j: (i, 0)),
        out_shape=jax.ShapeDtypeStruct((t, D_MODEL), _F32),
        scratch_shapes=[pltpu.VMEM((tm + 2 * HALO, D_MODEL), _BF16)],
        compiler_params=_params(("parallel", "arbitrary")),
        name=kind + "_block",
    )(*args)


def _pad_lanes(a, width):
    return jnp.pad(a, ((0, 0), (0, width - a.shape[1])))


def _attn_layer(x, pos_col, rel_table, layer_idx, norm_g, w_in, w_in_a, dq_g, dk_g, lq1, lk1, lq2, lk2, subln_g,
                q_a_g, w_uq, kv_a_g, w_ukv, mq_g, mk_g, w_out, *, batch, seq):
    row = lambda a: a.reshape(1, -1)
    lam_init = 0.8 - 0.6 * math.exp(-0.3 * layer_idx)
    hn = _norm(x, row(norm_g))

    reps = A_W // A_HALF_DIM
    gains = jnp.stack([jnp.tile(dq_g, reps) * (A_HALF_DIM ** -0.5 * LOG2E), jnp.tile(dk_g, reps)])
    qkv = _qkv_proj(hn, w_in_a, gains.reshape(2, 1, A_W))

    w_lat = _pad_lanes(w_in[:, 3 * A_W:], LAT_W).astype(_BF16)
    w_uq_p = jnp.pad(w_uq.reshape(B_Q_RANK, B_HEADS, B_QK_DIM),
                     ((0, 0), (0, 0), (0, B_QK_PAD - B_QK_DIM))).reshape(B_Q_RANK, -1).astype(_BF16)
    inv = 1.0 / (ROPE_THETA ** (jnp.arange(0, B_ROPE_DIM, 2, dtype=_F32) / B_ROPE_DIM))
    inv_lanes = _pad_lanes(jnp.tile(inv, 2).reshape(1, -1), LANES)
    qm, km, vm = _mla_prep(hn, pos_col, inv_lanes, w_lat, row(q_a_g), w_uq_p, row(kv_a_g),
                           w_ukv.astype(_BF16), _pad_lanes(row(mq_g), B_QK_PAD), _pad_lanes(row(mk_g), B_QK_PAD))

    band = _bias_band(rel_table, seq=seq, tq=ATTN_TQ)
    oa = _diff_attn(qkv, band, row(lq1), row(lk1), row(lq2), row(lk2), row(subln_g),
                    batch=batch, seq=seq, lam_init=lam_init, tq=ATTN_TQ)
    ob = _mla_attn(qm, km, vm, batch=batch, seq=seq, tq=ATTN_TQ)
    return _out_proj(x, oa, ob, w_out)


def kernel(x, positions, rel_bias_table, attn_norm_g, attn_w_in, diff_q_norm_g, diff_k_norm_g, diff_lambda_q1, diff_lambda_k1, diff_lambda_q2, diff_lambda_k2, diff_subln_g, mla_q_a_norm_g, mla_w_uq, mla_kv_a_norm_g, mla_w_ukv, mla_q_norm_g, mla_k_norm_g, attn_w_out, conv_norm_g, conv_w_in, conv_w, conv_w_out, ffn_norm_g, ffn_w_gate, ffn_w_up, ffn_dwconv_w, ffn_dwconv_b, ffn_w_down):
    batch, seq, d = x.shape
    depth = ffn_norm_g.shape[0]
    h = x.reshape(batch * seq, d)
    pos_col = positions.reshape(batch * seq, 1)
    for layer in range(depth):
        i = layer // 2
        if layer % 2 == 0:
            h = _attn_layer(h, pos_col, rel_bias_table, layer, attn_norm_g[i], attn_w_in[i],
                            _to_bf16(attn_w_in, i, cols=3 * A_W), diff_q_norm_g[i],
                            diff_k_norm_g[i], diff_lambda_q1[i], diff_lambda_k1[i], diff_lambda_q2[i],
                            diff_lambda_k2[i], diff_subln_g[i], mla_q_a_norm_g[i], mla_w_uq[i],
                            mla_kv_a_norm_g[i], mla_w_ukv[i], mla_q_norm_g[i], mla_k_norm_g[i],
                            _to_bf16(attn_w_out, i), batch=batch, seq=seq)
        else:
            h = _gated_block("conv", h, conv_norm_g[i].reshape(1, -1), (_to_bf16(conv_w_in, i),),
                             conv_w[i], None, _to_bf16(conv_w_out, i), seq=seq)
        h = _gated_block("ffn", h, ffn_norm_g[layer].reshape(1, -1),
                         (_to_bf16(ffn_w_gate, layer), _to_bf16(ffn_w_up, layer)),
                         ffn_dwconv_w[layer], ffn_dwconv_b[layer].reshape(1, -1),
                         _to_bf16(ffn_w_down, layer), seq=seq)
    return h.reshape(batch, seq, d)
```

```python
import functools
import math

import jax
import jax.numpy as jnp
from jax import lax
from jax.experimental import pallas as pl
from jax.experimental.pallas import tpu as pltpu

D_MODEL = 2048
A_HEADS = 8
A_HALF_DIM = 64
A_V_DIM = 128
B_HEADS = 8
B_NOPE_DIM = 128
B_ROPE_DIM = 64
B_QK_DIM = B_NOPE_DIM + B_ROPE_DIM
B_V_DIM = 128
B_Q_RANK = 512
B_KV_RANK = 256
B_QK_PAD = 256
ROPE_THETA = 10000.0
REL_BUCKETS = 32
REL_MAX_DIST = 128
EPS = 1e-6
A_W = A_HEADS * 2 * A_HALF_DIM
LAT_W = B_Q_RANK + B_KV_RANK + 128

LANES = 128
MXU_DIM = 256
HALO = 16
ATTN_TQ = 256
GATED_TM = {"ffn": 1024, "conv": 512}
GATED_SUB = 512
LOG2E = math.log2(math.e)
VMEM_LIMIT = 62 * 1024 * 1024

_F32 = jnp.float32
_BF16 = jnp.bfloat16


def _rms(xf, g):
    ms = jnp.mean(xf * xf, axis=-1, keepdims=True)
    return xf * lax.rsqrt(ms + EPS) * g


def _dot(a, b):
    return jnp.dot(a, b, preferred_element_type=_F32)


def _dot_nt(a, b):
    return lax.dot_general(a, b, (((1,), (1,)), ((), ())), preferred_element_type=_F32)


def _params(sem):
    return pltpu.CompilerParams(dimension_semantics=sem, vmem_limit_bytes=VMEM_LIMIT)


def _full(a):
    return pl.BlockSpec(a.shape, lambda *_: (0,) * a.ndim)


def _cast_kernel(w_ref, o_ref):
    o_ref[...] = w_ref[...].astype(_BF16)


def _to_bf16(w, layer, *, cols=None, tr=256):
    _, r, c = w.shape
    cols = c if cols is None else cols
    return pl.pallas_call(
        _cast_kernel,
        grid=(r // tr,),
        in_specs=[pl.BlockSpec((None, tr, cols), lambda i: (layer, i, 0))],
        out_specs=pl.BlockSpec((tr, cols), lambda i: (i, 0)),
        out_shape=jax.ShapeDtypeStruct((r, cols), _BF16),
        compiler_params=_params(("parallel",)),
        name="weight_to_bf16",
    )(w)


def _norm_kernel(x_ref, g_ref, pos_ref, inv_ref, o_ref, rope_ref):
    o_ref[...] = _rms(x_ref[...], g_ref[...]).astype(_BF16)
    ang = pos_ref[...].astype(_F32) * inv_ref[...]
    lane = lax.broadcasted_iota(jnp.int32, (1, LANES), 1)
    half = B_ROPE_DIM // 2
    cosv = jnp.cos(ang)
    sinv = jnp.sin(ang)
    rope_ref[0] = jnp.where(lane < B_ROPE_DIM, cosv, 0.0)
    rope_ref[1] = jnp.where(lane < half, -sinv, 0.0)
    rope_ref[2] = jnp.where((lane >= half) & (lane < B_ROPE_DIM), sinv, 0.0)


def _norm(x, g, pos, inv, *, tm=1024):
    t = x.shape[0]
    return pl.pallas_call(
        _norm_kernel,
        grid=(t // tm,),
        in_specs=[pl.BlockSpec((tm, D_MODEL), lambda i: (i, 0)), _full(g),
                  pl.BlockSpec((tm, 1), lambda i: (i, 0)), _full(inv)],
        out_specs=[pl.BlockSpec((tm, D_MODEL), lambda i: (i, 0)),
                   pl.BlockSpec((3, tm, LANES), lambda i: (0, i, 0))],
        out_shape=[jax.ShapeDtypeStruct((t, D_MODEL), _BF16), jax.ShapeDtypeStruct((3, t, LANES), _F32)],
        compiler_params=_params(("parallel",)),
        name="pre_norm",
    )(x, g, pos, inv)


def _qkv_proj_kernel(h_ref, w_ref, g_ref, bd_ref, o_ref):
    j = pl.program_id(0)
    h = h_ref[...]

    @pl.when(j < 2)
    def _():
        bd = bd_ref[...]
        y = _dot(h, w_ref[...])
        y2 = (y * y).astype(_BF16)
        width = bd.shape[0]
        for c in range(w_ref.shape[1] // width):
            cols = slice(c * width, (c + 1) * width)
            ms = _dot(y2[:, cols], bd) * (1.0 / A_HALF_DIM)
            o_ref[:, cols] = (y[:, cols] * lax.rsqrt(ms + EPS) * g_ref[0, :, cols]).astype(_BF16)

    @pl.when(j == 2)
    def _():
        o_ref[...] = _dot(h, w_ref[...]).astype(_BF16)


def _qkv_proj(h, w, gains, *, tm=512):
    t = h.shape[0]
    grp = jnp.arange(2 * MXU_DIM) // A_HALF_DIM
    bd = (grp[:, None] == grp[None, :]).astype(_BF16)
    return pl.pallas_call(
        _qkv_proj_kernel,
        grid=(3, t // tm),
        in_specs=[
            pl.BlockSpec((tm, D_MODEL), lambda j, i: (i, 0)),
            pl.BlockSpec((D_MODEL, A_W), lambda j, i: (0, j)),
            pl.BlockSpec((1, 1, A_W), lambda j, i: (jnp.minimum(j, 1), 0, 0)),
            _full(bd),
        ],
        out_specs=pl.BlockSpec((tm, A_W), lambda j, i: (i, j)),
        out_shape=jax.ShapeDtypeStruct((t, 3 * A_W), _BF16),
        compiler_params=_params(("arbitrary", "arbitrary")),
        name="diff_qkv_proj",
    )(h, w, gains, bd)


def _mla_prep_kernel(h_ref, rope_ref, wl_ref, qag_ref, wuq_ref, kvag_ref, wukv_ref, mqg_ref, mkg_ref,
                     q_ref, k_ref, v_ref, *, sub):
    half = B_ROPE_DIM // 2
    scale = B_QK_DIM ** -0.5 * LOG2E
    mqg = mqg_ref[...]
    mkg = mkg_ref[...]
    kv_w = B_NOPE_DIM + B_V_DIM

    def project(lo):
        lat = _dot(h_ref[lo:lo + sub], wl_ref[...])
        cq = lat[:, :B_Q_RANK]
        ckv = lat[:, B_Q_RANK:B_Q_RANK + B_KV_RANK]
        kr = lat[:, B_Q_RANK + B_KV_RANK:]
        qf = _dot(_rms(cq, qag_ref[...]).astype(_BF16), wuq_ref[...])
        kv = _dot(_rms(ckv, kvag_ref[...]).astype(_BF16), wukv_ref[...])
        return qf, kv, kr

    def finish(lo, qf, kv, kr):
        rows = slice(lo, lo + sub)
        c_tab, s_lo, s_hi = rope_ref[0, rows], rope_ref[1, rows], rope_ref[2, rows]

        def rope(r):
            return r * c_tab + pltpu.roll(r, LANES - half, 1) * s_lo + pltpu.roll(r, half, 1) * s_hi

        for h in range(B_HEADS):
            slab = qf[:, h * B_QK_PAD:(h + 1) * B_QK_PAD]
            ms = jnp.sum(slab * slab, axis=-1, keepdims=True) * (1.0 / B_QK_DIM)
            sn = slab * (lax.rsqrt(ms + EPS) * scale) * mqg
            q_ref[rows, h * B_QK_PAD:h * B_QK_PAD + B_NOPE_DIM] = sn[:, :B_NOPE_DIM].astype(_BF16)
            q_ref[rows, h * B_QK_PAD + B_NOPE_DIM:(h + 1) * B_QK_PAD] = rope(sn[:, B_NOPE_DIM:]).astype(_BF16)
        kr_ss = jnp.sum(kr * kr, axis=-1, keepdims=True)
        kr_rot = rope(kr * mkg[:, B_NOPE_DIM:])
        for h in range(B_HEADS):
            kn = kv[:, h * kv_w:h * kv_w + B_NOPE_DIM]
            ms = (jnp.sum(kn * kn, axis=-1, keepdims=True) + kr_ss) * (1.0 / B_QK_DIM)
            rs = lax.rsqrt(ms + EPS)
            k_ref[rows, h * B_QK_PAD:h * B_QK_PAD + B_NOPE_DIM] = (kn * rs * mkg[:, :B_NOPE_DIM]).astype(_BF16)
            k_ref[rows, h * B_QK_PAD + B_NOPE_DIM:(h + 1) * B_QK_PAD] = (kr_rot * rs).astype(_BF16)
            v_ref[rows, h * B_V_DIM:(h + 1) * B_V_DIM] = kv[:, h * kv_w + B_NOPE_DIM:(h + 1) * kv_w].astype(_BF16)

    starts = list(range(0, h_ref.shape[0], sub))
    nxt = project(starts[0])
    for n, lo in enumerate(starts):
        cur = nxt
        if n + 1 < len(starts):
            nxt = project(starts[n + 1])
        finish(lo, *cur)


def _mla_prep(h, rope_tabs, wl, qag, wuq, kvag, wukv, mqg, mkg, *, tm=1024, sub=256):
    t = h.shape[0]
    return pl.pallas_call(
        functools.partial(_mla_prep_kernel, sub=sub),
        grid=(t // tm,),
        in_specs=[
            pl.BlockSpec((tm, D_MODEL), lambda i: (i, 0)),
            pl.BlockSpec((3, tm, LANES), lambda i: (0, i, 0)),
            _full(wl), _full(qag), _full(wuq), _full(kvag), _full(wukv), _full(mqg), _full(mkg),
        ],
        out_specs=[
            pl.BlockSpec((tm, B_HEADS * B_QK_PAD), lambda i: (i, 0)),
            pl.BlockSpec((tm, B_HEADS * B_QK_PAD), lambda i: (i, 0)),
            pl.BlockSpec((tm, B_HEADS * B_V_DIM), lambda i: (i, 0)),
        ],
        out_shape=[
            jax.ShapeDtypeStruct((t, B_HEADS * B_QK_PAD), _BF16),
            jax.ShapeDtypeStruct((t, B_HEADS * B_QK_PAD), _BF16),
            jax.ShapeDtypeStruct((t, B_HEADS * B_V_DIM), _BF16),
        ],
        compiler_params=_params(("parallel",)),
        name="mla_prep",
    )(h, rope_tabs, wl, qag, wuq, kvag, wukv, mqg, mkg)


def _t5_bucket_of(rel):
    nb = REL_BUCKETS // 2
    max_exact = nb // 2
    n = jnp.abs(rel)
    large = jnp.full(rel.shape, max_exact, jnp.int32)
    for k in range(1, nb - max_exact):
        large = large + (n >= math.ceil(max_exact * 2.0 ** (k / 2.0))).astype(jnp.int32)
    return jnp.where(rel > 0, nb, 0) + jnp.where(n < max_exact, n, large)


def _bias_band_kernel(tbl_ref, o_ref, *, seq, tq):
    h = pl.program_id(0)
    blk = REL_MAX_DIST
    row = lax.broadcasted_iota(jnp.int32, (blk, blk), 0)
    col = lax.broadcasted_iota(jnp.int32, (blk, blk), 1)
    tiles = {}
    for d in range(-2, 3):
        rel = d * blk + col - row if abs(d) < 2 else jnp.full((blk, blk), d * blk // 2, jnp.int32)
        bucket = _t5_bucket_of(rel)
        val = jnp.zeros((blk, blk), _F32)
        for c in range(REL_BUCKETS):
            val = jnp.where(bucket == c, tbl_ref[c * A_HEADS + h], val)
        tiles[d] = val * LOG2E
    shift = (seq - tq) // blk
    for rb in range(tq // blk):
        for cb in range((2 * seq - tq) // blk):
            d = max(-2, min(2, cb - rb - shift))
            o_ref[0, rb * blk:(rb + 1) * blk, cb * blk:(cb + 1) * blk] = tiles[d]


def _bias_band(rel_table, *, seq, tq):
    cols = 2 * seq - tq
    return pl.pallas_call(
        functools.partial(_bias_band_kernel, seq=seq, tq=tq),
        grid=(A_HEADS,),
        in_specs=[pl.BlockSpec(memory_space=pltpu.SMEM)],
        out_specs=pl.BlockSpec((1, tq, cols), lambda h: (h, 0, 0)),
        out_shape=jax.ShapeDtypeStruct((A_HEADS, tq, cols), _F32),
        compiler_params=_params(("arbitrary",)),
        name="rel_bias_band",
    )(rel_table.reshape(-1))


def _softmax_pv(s, va):
    m = jnp.max(s, axis=-1, keepdims=True)
    p = jnp.exp2(s - m).astype(_BF16)
    oa = _dot(p, va)
    dv = va.shape[1] // 2
    return oa[:, :dv] / oa[:, dv:]


def _stage_v(v_ref, va_ref):
    dv = v_ref.shape[1]
    va_ref[:, :dv] = v_ref[...]
    va_ref[:, dv:] = jnp.ones(v_ref.shape, _BF16)


def _diff_attn_kernel(q_ref, k_ref, v_ref, band_ref, lq1_ref, lk1_ref, lq2_ref, lk2_ref, sg_ref, o_ref,
                      va_ref, *, tq, seq, lam_init):
    _stage_v(v_ref, va_ref)
    k = k_ref[...]
    va = va_ref[...]
    lane = lax.broadcasted_iota(jnp.int32, (1, LANES), 1)
    lam = (jnp.exp(jnp.sum(lq1_ref[...] * lk1_ref[...], axis=-1, keepdims=True))
           - jnp.exp(jnp.sum(lq2_ref[...] * lk2_ref[...], axis=-1, keepdims=True)) + lam_init)
    sg = sg_ref[...] * (1.0 - lam_init)
    n = seq // tq

    def logits(t, half):
        q = q_ref[t * tq:(t + 1) * tq]
        keep = (lane >= A_HALF_DIM) if half else (lane < A_HALF_DIM)
        off = seq - (t + 1) * tq
        return _dot_nt(jnp.where(keep, q, jnp.zeros_like(q)), k) + band_ref[0, :, off:off + seq]

    units = [(t, half) for t in range(n) for half in range(2)]
    s_next = logits(*units[0])
    for u, (t, half) in enumerate(units):
        s = s_next
        if u + 1 < len(units):
            s_next = logits(*units[u + 1])
        if half == 0:
            o1 = _softmax_pv(s, va)
        else:
            o = o1 - lam * _softmax_pv(s, va)
            o_ref[t * tq:(t + 1) * tq] = _rms(o, sg).astype(_BF16)


def _diff_attn(qkv, band, lq1, lk1, lq2, lk2, sg, *, batch, seq, lam_init, tq):
    return pl.pallas_call(
        functools.partial(_diff_attn_kernel, tq=tq, seq=seq, lam_init=lam_init),
        grid=(A_HEADS, batch),
        in_specs=[
            pl.BlockSpec((seq, LANES), lambda h, b: (b, h)),
            pl.BlockSpec((seq, LANES), lambda h, b: (b, A_HEADS + h)),
            pl.BlockSpec((seq, LANES), lambda h, b: (b, 2 * A_HEADS + h)),
            pl.BlockSpec((1, tq, band.shape[2]), lambda h, b: (h, 0, 0)),
            _full(lq1), _full(lk1), _full(lq2), _full(lk2), _full(sg),
        ],
        out_specs=pl.BlockSpec((seq, A_V_DIM), lambda h, b: (b, h)),
        out_shape=jax.ShapeDtypeStruct((batch * seq, A_HEADS * A_V_DIM), _BF16),
        scratch_shapes=[pltpu.VMEM((seq, 2 * A_V_DIM), _BF16)],
        compiler_params=_params(("parallel", "arbitrary")),
        name="diff_attn",
    )(qkv, qkv, qkv, band, lq1, lk1, lq2, lk2, sg)


def _mla_attn_kernel(q_ref, k_ref, v_ref, o_ref, va_ref, *, tq, seq):
    _stage_v(v_ref, va_ref)
    k = k_ref[...]
    va = va_ref[...]
    n = seq // tq
    s_next = _dot_nt(q_ref[0:tq], k)
    for t in range(n):
        s = s_next
        if t + 1 < n:
            s_next = _dot_nt(q_ref[(t + 1) * tq:(t + 2) * tq], k)
        o_ref[t * tq:(t + 1) * tq] = _softmax_pv(s, va).astype(_BF16)


def _mla_attn(q, k, v, *, batch, seq, tq):
    return pl.pallas_call(
        functools.partial(_mla_attn_kernel, tq=tq, seq=seq),
        grid=(B_HEADS, batch),
        in_specs=[
            pl.BlockSpec((seq, B_QK_PAD), lambda h, b: (b, h)),
            pl.BlockSpec((seq, B_QK_PAD), lambda h, b: (b, h)),
            pl.BlockSpec((seq, B_V_DIM), lambda h, b: (b, h)),
        ],
        out_specs=pl.BlockSpec((seq, B_V_DIM), lambda h, b: (b, h)),
        out_shape=jax.ShapeDtypeStruct((batch * seq, B_HEADS * B_V_DIM), _BF16),
        scratch_shapes=[pltpu.VMEM((seq, 2 * B_V_DIM), _BF16)],
        compiler_params=_params(("parallel", "arbitrary")),
        name="mla_attn",
    )(q, k, v)


def _out_proj_kernel(x_ref, oa_ref, ob_ref, wa_ref, wb_ref, o_ref):
    o_ref[...] = x_ref[...] + _dot(oa_ref[...], wa_ref[...]) + _dot(ob_ref[...], wb_ref[...])


def _out_proj(x, oa, ob, w, *, tm=512):
    t = x.shape[0]
    ka, kb = oa.shape[1], ob.shape[1]
    assert ka == kb and ka + kb == w.shape[0]
    return pl.pallas_call(
        _out_proj_kernel,
        grid=(t // tm,),
        in_specs=[
            pl.BlockSpec((tm, D_MODEL), lambda i: (i, 0)),
            pl.BlockSpec((tm, ka), lambda i: (i, 0)),
            pl.BlockSpec((tm, kb), lambda i: (i, 0)),
            pl.BlockSpec((ka, D_MODEL), lambda i: (0, 0)),
            pl.BlockSpec((kb, D_MODEL), lambda i: (1, 0)),
        ],
        out_specs=pl.BlockSpec((tm, D_MODEL), lambda i: (i, 0)),
        out_shape=jax.ShapeDtypeStruct((t, D_MODEL), _F32),
        compiler_params=_params(("parallel",)),
        name="attn_out_proj",
    )(x, oa, ob, w, w)


def _dwconv3(g_ext, cw, tm):
    n = g_ext.shape[0]
    prev = pltpu.roll(g_ext, 1, 0)[HALO:HALO + tm]
    nxt = pltpu.roll(g_ext, n - 1, 0)[HALO:HALO + tm]
    return cw[0:1] * prev + cw[1:2] * g_ext[HALO:HALO + tm] + cw[2:3] * nxt


def _gated_block_kernel(*refs, kind, tm, seq):
    if kind == "ffn":
        x_ref, xp_ref, xn_ref, g_ref, wa_ref, wb_ref, cw_ref, cb_ref, wd_ref, o_ref, hn_ref = refs
    else:
        x_ref, xp_ref, xn_ref, g_ref, wa_ref, wb_ref, wc_ref, cw_ref, wd_ref, o_ref, hn_ref = refs
    i = pl.program_id(0)
    j = pl.program_id(1)
    tiles_per_seq = seq // tm

    @pl.when(j == 0)
    def _():
        g = g_ref[...]
        pos_in_seq = i % tiles_per_seq
        keep_prev = (pos_in_seq != 0).astype(_F32)
        keep_next = (pos_in_seq != tiles_per_seq - 1).astype(_F32)
        hn_ref[0:HALO] = (_rms(xp_ref[...], g) * keep_prev).astype(_BF16)
        hn_ref[HALO:HALO + tm] = _rms(x_ref[...], g).astype(_BF16)
        hn_ref[HALO + tm:] = (_rms(xn_ref[...], g) * keep_next).astype(_BF16)
        o_ref[...] = x_ref[...]

    cw = cw_ref[...]
    sub = min(tm, GATED_SUB)
    for lo in range(0, tm, sub):
        h_ext = hn_ref[lo:lo + sub + 2 * HALO]
        h_main = hn_ref[lo + HALO:lo + HALO + sub]
        if kind == "ffn":
            c = _dwconv3(_dot(h_ext, wa_ref[...]), cw, sub) + cb_ref[...]
            act = c * jax.nn.sigmoid(c) * _dot(h_main, wb_ref[...])
        else:
            act = _dot(h_main, wa_ref[...]) * _dwconv3(_dot(h_ext, wb_ref[...]) * _dot(h_ext, wc_ref[...]), cw, sub)
        o_ref[lo:lo + sub] += _dot(act.astype(_BF16), wd_ref[...])


def _gated_block(kind, x, g, ups, cw, cb, wd, *, seq, tc=512):
    t = x.shape[0]
    tm = GATED_TM[kind]
    c_total = wd.shape[0]
    nc = c_total // tc
    hb = tm // HALO
    row_specs = [
        pl.BlockSpec((tm, D_MODEL), lambda i, j: (i, 0)),
        pl.BlockSpec((HALO, D_MODEL), lambda i, j: (jnp.maximum(i * hb - 1, 0), 0)),
        pl.BlockSpec((HALO, D_MODEL), lambda i, j: (jnp.minimum((i + 1) * hb, t // HALO - 1), 0)),
        pl.BlockSpec((1, D_MODEL), lambda i, j: (0, 0)),
    ]
    if kind == "ffn":
        wg, wu = ups
        args = (x, x, x, g, wg, wu, cw, cb, wd)
        w_specs = [
            pl.BlockSpec((D_MODEL, tc), lambda i, j: (0, j)),
            pl.BlockSpec((D_MODEL, tc), lambda i, j: (0, j)),
            pl.BlockSpec((3, tc), lambda i, j: (0, j)),
            pl.BlockSpec((1, tc), lambda i, j: (0, j)),
        ]
    else:
        (w_in,) = ups
        args = (x, x, x, g, w_in, w_in, w_in, cw, wd)
        w_specs = [
            pl.BlockSpec((D_MODEL, tc), lambda i, j: (0, j)),
            pl.BlockSpec((D_MODEL, tc), lambda i, j: (0, nc + j)),
            pl.BlockSpec((D_MODEL, tc), lambda i, j: (0, 2 * nc + j)),
            pl.BlockSpec((3, tc), lambda i, j: (0, j)),
        ]
    return pl.pallas_call(
        functools.partial(_gated_block_kernel, kind=kind, tm=tm, seq=seq),
        grid=(t // tm, nc),
        in_specs=row_specs + w_specs + [pl.BlockSpec((tc, D_MODEL), lambda i, j: (j, 0))],
        out_specs=pl.BlockSpec((tm, D_MODEL), lambda i, j: (i, 0)),
        out_shape=jax.ShapeDtypeStruct((t, D_MODEL), _F32),
        scratch_shapes=[pltpu.VMEM((tm + 2 * HALO, D_MODEL), _BF16)],
        compiler_params=_params(("parallel", "arbitrary")),
        name=kind + "_block",
    )(*args)


def _pad_lanes(a, width):
    return jnp.pad(a, ((0, 0), (0, width - a.shape[1])))


def _attn_layer(x, pos_col, rel_table, layer_idx, norm_g, w_in, w_in_a, dq_g, dk_g, lq1, lk1, lq2, lk2, subln_g,
                q_a_g, w_uq, kv_a_g, w_ukv, mq_g, mk_g, w_out, *, batch, seq):
    row = lambda a: a.reshape(1, -1)
    lam_init = 0.8 - 0.6 * math.exp(-0.3 * layer_idx)
    inv = 1.0 / (ROPE_THETA ** (jnp.arange(0, B_ROPE_DIM, 2, dtype=_F32) / B_ROPE_DIM))
    hn, rope_tabs = _norm(x, row(norm_g), pos_col, _pad_lanes(jnp.tile(inv, 2).reshape(1, -1), LANES))

    reps = A_W // A_HALF_DIM
    gains = jnp.stack([jnp.tile(dq_g, reps) * (A_HALF_DIM ** -0.5 * LOG2E), jnp.tile(dk_g, reps)])
    qkv = _qkv_proj(hn, w_in_a, gains.reshape(2, 1, A_W))

    w_lat = _pad_lanes(w_in[:, 3 * A_W:], LAT_W).astype(_BF16)
    w_uq_p = jnp.pad(w_uq.reshape(B_Q_RANK, B_HEADS, B_QK_DIM),
                     ((0, 0), (0, 0), (0, B_QK_PAD - B_QK_DIM))).reshape(B_Q_RANK, -1).astype(_BF16)
    qm, km, vm = _mla_prep(hn, rope_tabs, w_lat, row(q_a_g), w_uq_p, row(kv_a_g), w_ukv.astype(_BF16),
                           _pad_lanes(row(mq_g), B_QK_PAD), _pad_lanes(row(mk_g), B_QK_PAD))

    band = _bias_band(rel_table, seq=seq, tq=ATTN_TQ)
    oa = _diff_attn(qkv, band, row(lq1), row(lk1), row(lq2), row(lk2), row(subln_g),
                    batch=batch, seq=seq, lam_init=lam_init, tq=ATTN_TQ)
    ob = _mla_attn(qm, km, vm, batch=batch, seq=seq, tq=ATTN_TQ)
    return _out_proj(x, oa, ob, w_out)


def kernel(x, positions, rel_bias_table, attn_norm_g, attn_w_in, diff_q_norm_g, diff_k_norm_g, diff_lambda_q1, diff_lambda_k1, diff_lambda_q2, diff_lambda_k2, diff_subln_g, mla_q_a_norm_g, mla_w_uq, mla_kv_a_norm_g, mla_w_ukv, mla_q_norm_g, mla_k_norm_g, attn_w_out, conv_norm_g, conv_w_in, conv_w, conv_w_out, ffn_norm_g, ffn_w_gate, ffn_w_up, ffn_dwconv_w, ffn_dwconv_b, ffn_w_down):
    batch, seq, d = x.shape
    depth = ffn_norm_g.shape[0]
    h = x.reshape(batch * seq, d)
    pos_col = positions.reshape(batch * seq, 1)
    for layer in range(depth):
        i = layer // 2
        if layer % 2 == 0:
            h = _attn_layer(h, pos_col, rel_bias_table, layer, attn_norm_g[i], attn_w_in[i],
                            _to_bf16(attn_w_in, i, cols=3 * A_W), diff_q_norm_g[i],
                            diff_k_norm_g[i], diff_lambda_q1[i], diff_lambda_k1[i], diff_lambda_q2[i],
                            diff_lambda_k2[i], diff_subln_g[i], mla_q_a_norm_g[i], mla_w_uq[i],
                            mla_kv_a_norm_g[i], mla_w_ukv[i], mla_q_norm_g[i], mla_k_norm_g[i],
                            _to_bf16(attn_w_out, i), batch=batch, seq=seq)
        else:
            h = _gated_block("conv", h, conv_norm_g[i].reshape(1, -1), (_to_bf16(conv_w_in, i),),
                             conv_w[i], None, _to_bf16(conv_w_out, i), seq=seq)
        h = _gated_block("ffn", h, ffn_norm_g[layer].reshape(1, -1),
                         (_to_bf16(ffn_w_gate, layer), _to_bf16(ffn_w_up, layer)),
                         ffn_dwconv_w[layer], ffn_dwconv_b[layer].reshape(1, -1),
                         _to_bf16(ffn_w_down, layer), seq=seq)
    return h.reshape(batch, seq, d)
```

```python
import functools
import math

import jax
import jax.numpy as jnp
from jax import lax
from jax.experimental import pallas as pl
from jax.experimental.pallas import tpu as pltpu

D_MODEL = 2048
A_HEADS = 8
A_HALF_DIM = 64
A_V_DIM = 128
B_HEADS = 8
B_NOPE_DIM = 128
B_ROPE_DIM = 64
B_QK_DIM = B_NOPE_DIM + B_ROPE_DIM
B_V_DIM = 128
B_Q_RANK = 512
B_KV_RANK = 256
B_QK_PAD = 256
ROPE_THETA = 10000.0
REL_BUCKETS = 32
REL_MAX_DIST = 128
EPS = 1e-6
A_W = A_HEADS * 2 * A_HALF_DIM
LAT_W = B_Q_RANK + B_KV_RANK + 128

LANES = 128
MXU_DIM = 256
HALO = 16
ATTN_TQ = 256
ATTN_HEAD_GROUP = 2
GATED_TM = {"ffn": 1024, "conv": 512}
GATED_SUB = 512
LOG2E = math.log2(math.e)
VMEM_LIMIT = 62 * 1024 * 1024

_F32 = jnp.float32
_BF16 = jnp.bfloat16


def _rms(xf, g):
    ms = jnp.mean(xf * xf, axis=-1, keepdims=True)
    return xf * lax.rsqrt(ms + EPS) * g


def _dot(a, b):
    return jnp.dot(a, b, preferred_element_type=_F32)


def _dot_nt(a, b):
    return lax.dot_general(a, b, (((1,), (1,)), ((), ())), preferred_element_type=_F32)


def _params(sem):
    return pltpu.CompilerParams(dimension_semantics=sem, vmem_limit_bytes=VMEM_LIMIT)


def _full(a):
    return pl.BlockSpec(a.shape, lambda *_: (0,) * a.ndim)


def _cast_kernel(w_ref, o_ref):
    o_ref[...] = w_ref[...].astype(_BF16)


def _to_bf16(w, layer, *, tr=256):
    _, r, c = w.shape
    return pl.pallas_call(
        _cast_kernel,
        grid=(r // tr,),
        in_specs=[pl.BlockSpec((None, tr, c), lambda i: (layer, i, 0))],
        out_specs=pl.BlockSpec((tr, c), lambda i: (i, 0)),
        out_shape=jax.ShapeDtypeStruct((r, c), _BF16),
        compiler_params=_params(("parallel",)),
        name="weight_to_bf16",
    )(w)


def _norm_kernel(x_ref, g_ref, pos_ref, inv_ref, o_ref, rope_ref):
    o_ref[...] = _rms(x_ref[...], g_ref[...]).astype(_BF16)
    ang = pos_ref[...].astype(_F32) * inv_ref[...]
    lane = lax.broadcasted_iota(jnp.int32, (1, LANES), 1)
    half = B_ROPE_DIM // 2
    cosv = jnp.cos(ang)
    sinv = jnp.sin(ang)
    rope_ref[0] = jnp.where(lane < B_ROPE_DIM, cosv, 0.0)
    rope_ref[1] = jnp.where(lane < half, -sinv, 0.0)
    rope_ref[2] = jnp.where((lane >= half) & (lane < B_ROPE_DIM), sinv, 0.0)


def _norm(x, g, pos, inv, *, tm=1024):
    t = x.shape[0]
    return pl.pallas_call(
        _norm_kernel,
        grid=(t // tm,),
        in_specs=[pl.BlockSpec((tm, D_MODEL), lambda i: (i, 0)), _full(g),
                  pl.BlockSpec((tm, 1), lambda i: (i, 0)), _full(inv)],
        out_specs=[pl.BlockSpec((tm, D_MODEL), lambda i: (i, 0)),
                   pl.BlockSpec((3, tm, LANES), lambda i: (0, i, 0))],
        out_shape=[jax.ShapeDtypeStruct((t, D_MODEL), _BF16), jax.ShapeDtypeStruct((3, t, LANES), _F32)],
        compiler_params=_params(("parallel",)),
        name="pre_norm",
    )(x, g, pos, inv)


def _qkv_proj_kernel(h_ref, w32_ref, g_ref, bd_ref, o_ref, w_ref):
    j = pl.program_id(0)
    h = h_ref[...]

    @pl.when(pl.program_id(1) == 0)
    def _():
        w_ref[...] = w32_ref[...].astype(_BF16)

    @pl.when(j < 2)
    def _():
        bd = bd_ref[...]
        y = _dot_nt(h, w_ref[...])
        y2 = (y * y).astype(_BF16)
        width = bd.shape[0]
        for c in range(w_ref.shape[0] // width):
            cols = slice(c * width, (c + 1) * width)
            ms = _dot(y2[:, cols], bd) * (1.0 / A_HALF_DIM)
            o_ref[:, cols] = (y[:, cols] * lax.rsqrt(ms + EPS) * g_ref[0, :, cols]).astype(_BF16)

    @pl.when(j == 2)
    def _():
        o_ref[...] = _dot_nt(h, w_ref[...]).astype(_BF16)


def _qkv_proj(h, wt, layer, gains, *, tm=512):
    t = h.shape[0]
    grp = jnp.arange(2 * MXU_DIM) // A_HALF_DIM
    bd = (grp[:, None] == grp[None, :]).astype(_BF16)
    return pl.pallas_call(
        _qkv_proj_kernel,
        grid=(3, t // tm),
        in_specs=[
            pl.BlockSpec((tm, D_MODEL), lambda j, i: (i, 0)),
            pl.BlockSpec((None, A_W, D_MODEL), lambda j, i: (layer, j, 0)),
            pl.BlockSpec((1, 1, A_W), lambda j, i: (jnp.minimum(j, 1), 0, 0)),
            _full(bd),
        ],
        out_specs=pl.BlockSpec((tm, A_W), lambda j, i: (i, j)),
        out_shape=jax.ShapeDtypeStruct((t, 3 * A_W), _BF16),
        scratch_shapes=[pltpu.VMEM((A_W, D_MODEL), _BF16)],
        compiler_params=_params(("arbitrary", "arbitrary")),
        name="diff_qkv_proj",
    )(h, wt, gains, bd)


def _mla_prep_kernel(h_ref, rope_ref, wl_ref, qag_ref, wuq_ref, kvag_ref, wukv_ref, mqg_ref, mkg_ref,
                     q_ref, k_ref, v_ref, *, sub):
    half = B_ROPE_DIM // 2
    scale = B_QK_DIM ** -0.5 * LOG2E
    mqg = mqg_ref[...]
    mkg = mkg_ref[...]
    kv_w = B_NOPE_DIM + B_V_DIM

    def project(lo):
        lat = _dot_nt(h_ref[lo:lo + sub], wl_ref[...])
        cq = lat[:, :B_Q_RANK]
        ckv = lat[:, B_Q_RANK:B_Q_RANK + B_KV_RANK]
        kr = lat[:, B_Q_RANK + B_KV_RANK:]
        qf = _dot(_rms(cq, qag_ref[...]).astype(_BF16), wuq_ref[...])
        kv = _dot(_rms(ckv, kvag_ref[...]).astype(_BF16), wukv_ref[...])
        return qf, kv, kr

    def finish(lo, qf, kv, kr):
        rows = slice(lo, lo + sub)
        c_tab, s_lo, s_hi = rope_ref[0, rows], rope_ref[1, rows], rope_ref[2, rows]

        def rope(r):
            return r * c_tab + pltpu.roll(r, LANES - half, 1) * s_lo + pltpu.roll(r, half, 1) * s_hi

        for h in range(B_HEADS):
            slab = qf[:, h * B_QK_PAD:(h + 1) * B_QK_PAD]
            ms = jnp.sum(slab * slab, axis=-1, keepdims=True) * (1.0 / B_QK_DIM)
            sn = slab * (lax.rsqrt(ms + EPS) * scale) * mqg
            q_ref[rows, h * B_QK_PAD:h * B_QK_PAD + B_NOPE_DIM] = sn[:, :B_NOPE_DIM].astype(_BF16)
            q_ref[rows, h * B_QK_PAD + B_NOPE_DIM:(h + 1) * B_QK_PAD] = rope(sn[:, B_NOPE_DIM:]).astype(_BF16)
        kr_ss = jnp.sum(kr * kr, axis=-1, keepdims=True)
        kr_rot = rope(kr * mkg[:, B_NOPE_DIM:])
        for h in range(B_HEADS):
            kn = kv[:, h * kv_w:h * kv_w + B_NOPE_DIM]
            ms = (jnp.sum(kn * kn, axis=-1, keepdims=True) + kr_ss) * (1.0 / B_QK_DIM)
            rs = lax.rsqrt(ms + EPS)
            k_ref[rows, h * B_QK_PAD:h * B_QK_PAD + B_NOPE_DIM] = (kn * rs * mkg[:, :B_NOPE_DIM]).astype(_BF16)
            k_ref[rows, h * B_QK_PAD + B_NOPE_DIM:(h + 1) * B_QK_PAD] = (kr_rot * rs).astype(_BF16)
            v_ref[rows, h * B_V_DIM:(h + 1) * B_V_DIM] = kv[:, h * kv_w + B_NOPE_DIM:(h + 1) * kv_w].astype(_BF16)

    starts = list(range(0, h_ref.shape[0], sub))
    nxt = project(starts[0])
    for n, lo in enumerate(starts):
        cur = nxt
        if n + 1 < len(starts):
            nxt = project(starts[n + 1])
        finish(lo, *cur)


def _mla_prep(h, rope_tabs, wl, qag, wuq, kvag, wukv, mqg, mkg, *, tm=1024, sub=256):
    t = h.shape[0]
    return pl.pallas_call(
        functools.partial(_mla_prep_kernel, sub=sub),
        grid=(t // tm,),
        in_specs=[
            pl.BlockSpec((tm, D_MODEL), lambda i: (i, 0)),
            pl.BlockSpec((3, tm, LANES), lambda i: (0, i, 0)),
            _full(wl), _full(qag), _full(wuq), _full(kvag), _full(wukv), _full(mqg), _full(mkg),
        ],
        out_specs=[
            pl.BlockSpec((tm, B_HEADS * B_QK_PAD), lambda i: (i, 0)),
            pl.BlockSpec((tm, B_HEADS * B_QK_PAD), lambda i: (i, 0)),
            pl.BlockSpec((tm, B_HEADS * B_V_DIM), lambda i: (i, 0)),
        ],
        out_shape=[
            jax.ShapeDtypeStruct((t, B_HEADS * B_QK_PAD), _BF16),
            jax.ShapeDtypeStruct((t, B_HEADS * B_QK_PAD), _BF16),
            jax.ShapeDtypeStruct((t, B_HEADS * B_V_DIM), _BF16),
        ],
        compiler_params=_params(("parallel",)),
        name="mla_prep",
    )(h, rope_tabs, wl, qag, wuq, kvag, wukv, mqg, mkg)


def _t5_bucket_of(rel):
    nb = REL_BUCKETS // 2
    max_exact = nb // 2
    n = jnp.abs(rel)
    large = jnp.full(rel.shape, max_exact, jnp.int32)
    for k in range(1, nb - max_exact):
        large = large + (n >= math.ceil(max_exact * 2.0 ** (k / 2.0))).astype(jnp.int32)
    return jnp.where(rel > 0, nb, 0) + jnp.where(n < max_exact, n, large)


def _bias_band_kernel(tbl_ref, o_ref, *, seq, tq):
    h = pl.program_id(0)
    blk = REL_MAX_DIST
    row = lax.broadcasted_iota(jnp.int32, (blk, blk), 0)
    col = lax.broadcasted_iota(jnp.int32, (blk, blk), 1)
    tiles = {}
    for d in range(-2, 3):
        rel = d * blk + col - row if abs(d) < 2 else jnp.full((blk, blk), d * blk // 2, jnp.int32)
        bucket = _t5_bucket_of(rel)
        val = jnp.zeros((blk, blk), _F32)
        for c in range(REL_BUCKETS):
            val = jnp.where(bucket == c, tbl_ref[c * A_HEADS + h], val)
        tiles[d] = val * LOG2E
    shift = (seq - tq) // blk
    for rb in range(tq // blk):
        for cb in range((2 * seq - tq) // blk):
            d = max(-2, min(2, cb - rb - shift))
            o_ref[0, rb * blk:(rb + 1) * blk, cb * blk:(cb + 1) * blk] = tiles[d]


def _bias_band(rel_table, *, seq, tq):
    cols = 2 * seq - tq
    return pl.pallas_call(
        functools.partial(_bias_band_kernel, seq=seq, tq=tq),
        grid=(A_HEADS,),
        in_specs=[pl.BlockSpec(memory_space=pltpu.SMEM)],
        out_specs=pl.BlockSpec((1, tq, cols), lambda h: (h, 0, 0)),
        out_shape=jax.ShapeDtypeStruct((A_HEADS, tq, cols), _F32),
        compiler_params=_params(("arbitrary",)),
        name="rel_bias_band",
    )(rel_table.reshape(-1))


def _softmax_pv(s, va):
    m = jnp.max(s, axis=-1, keepdims=True)
    p = jnp.exp2(s - m).astype(_BF16)
    oa = _dot(p, va)
    dv = va.shape[1] // 2
    return oa[:, :dv] / oa[:, dv:]


def _stage_v(v_ref, va_ref, dv):
    for g in range(va_ref.shape[0]):
        va_ref[g, :, :dv] = v_ref[:, g * dv:(g + 1) * dv]
        va_ref[g, :, dv:] = jnp.ones((v_ref.shape[0], dv), _BF16)


def _diff_attn_kernel(q_ref, k_ref, v_ref, band_ref, lq1_ref, lk1_ref, lq2_ref, lk2_ref, sg_ref, o_ref,
                      va_ref, *, tq, seq, lam_init):
    _stage_v(v_ref, va_ref, A_V_DIM)
    lane = lax.broadcasted_iota(jnp.int32, (1, LANES), 1)
    lam = (jnp.exp(jnp.sum(lq1_ref[...] * lk1_ref[...], axis=-1, keepdims=True))
           - jnp.exp(jnp.sum(lq2_ref[...] * lk2_ref[...], axis=-1, keepdims=True)) + lam_init)
    sg = sg_ref[...] * (1.0 - lam_init)

    def logits(g, t, half):
        cols = slice(g * LANES, (g + 1) * LANES)
        q = q_ref[t * tq:(t + 1) * tq, cols]
        keep = (lane >= A_HALF_DIM) if half else (lane < A_HALF_DIM)
        off = seq - (t + 1) * tq
        return (_dot_nt(jnp.where(keep, q, jnp.zeros_like(q)), k_ref[:, cols])
                + band_ref[g, :, off:off + seq])

    units = [(g, t, half) for g in range(va_ref.shape[0]) for t in range(seq // tq) for half in range(2)]
    s_next = logits(*units[0])
    for u, (g, t, half) in enumerate(units):
        s = s_next
        if u + 1 < len(units):
            s_next = logits(*units[u + 1])
        if half == 0:
            o1 = _softmax_pv(s, va_ref[g])
        else:
            o = o1 - lam * _softmax_pv(s, va_ref[g])
            o_ref[t * tq:(t + 1) * tq, g * A_V_DIM:(g + 1) * A_V_DIM] = _rms(o, sg).astype(_BF16)


def _diff_attn(qkv, band, lq1, lk1, lq2, lk2, sg, *, batch, seq, lam_init, tq, hg=ATTN_HEAD_GROUP):
    groups = A_HEADS // hg
    w = hg * LANES
    return pl.pallas_call(
        functools.partial(_diff_attn_kernel, tq=tq, seq=seq, lam_init=lam_init),
        grid=(groups, batch),
        in_specs=[
            pl.BlockSpec((seq, w), lambda h, b: (b, h)),
            pl.BlockSpec((seq, w), lambda h, b: (b, groups + h)),
            pl.BlockSpec((seq, w), lambda h, b: (b, 2 * groups + h)),
            pl.BlockSpec((hg, tq, band.shape[2]), lambda h, b: (h, 0, 0)),
            _full(lq1), _full(lk1), _full(lq2), _full(lk2), _full(sg),
        ],
        out_specs=pl.BlockSpec((seq, hg * A_V_DIM), lambda h, b: (b, h)),
        out_shape=jax.ShapeDtypeStruct((batch * seq, A_HEADS * A_V_DIM), _BF16),
        scratch_shapes=[pltpu.VMEM((hg, seq, 2 * A_V_DIM), _BF16)],
        compiler_params=_params(("parallel", "arbitrary")),
        name="diff_attn",
    )(qkv, qkv, qkv, band, lq1, lk1, lq2, lk2, sg)


def _mla_attn_kernel(q_ref, k_ref, v_ref, o_ref, va_ref, *, tq, seq):
    _stage_v(v_ref, va_ref, B_V_DIM)

    def logits(g, t):
        cols = slice(g * B_QK_PAD, (g + 1) * B_QK_PAD)
        return _dot_nt(q_ref[t * tq:(t + 1) * tq, cols], k_ref[:, cols])

    units = [(g, t) for g in range(va_ref.shape[0]) for t in range(seq // tq)]
    s_next = logits(*units[0])
    for u, (g, t) in enumerate(units):
        s = s_next
        if u + 1 < len(units):
            s_next = logits(*units[u + 1])
        o_ref[t * tq:(t + 1) * tq, g * B_V_DIM:(g + 1) * B_V_DIM] = _softmax_pv(s, va_ref[g]).astype(_BF16)


def _mla_attn(q, k, v, *, batch, seq, tq, hg=ATTN_HEAD_GROUP):
    return pl.pallas_call(
        functools.partial(_mla_attn_kernel, tq=tq, seq=seq),
        grid=(B_HEADS // hg, batch),
        in_specs=[
            pl.BlockSpec((seq, hg * B_QK_PAD), lambda h, b: (b, h)),
            pl.BlockSpec((seq, hg * B_QK_PAD), lambda h, b: (b, h)),
            pl.BlockSpec((seq, hg * B_V_DIM), lambda h, b: (b, h)),
        ],
        out_specs=pl.BlockSpec((seq, hg * B_V_DIM), lambda h, b: (b, h)),
        out_shape=jax.ShapeDtypeStruct((batch * seq, B_HEADS * B_V_DIM), _BF16),
        scratch_shapes=[pltpu.VMEM((hg, seq, 2 * B_V_DIM), _BF16)],
        compiler_params=_params(("parallel", "arbitrary")),
        name="mla_attn",
    )(q, k, v)


def _out_proj_kernel(x_ref, oa_ref, ob_ref, wa_ref, wb_ref, o_ref):
    o_ref[...] = x_ref[...] + _dot(oa_ref[...], wa_ref[...]) + _dot(ob_ref[...], wb_ref[...])


def _out_proj(x, oa, ob, w, *, tm=512):
    t = x.shape[0]
    ka, kb = oa.shape[1], ob.shape[1]
    assert ka == kb and ka + kb == w.shape[0]
    return pl.pallas_call(
        _out_proj_kernel,
        grid=(t // tm,),
        in_specs=[
            pl.BlockSpec((tm, D_MODEL), lambda i: (i, 0)),
            pl.BlockSpec((tm, ka), lambda i: (i, 0)),
            pl.BlockSpec((tm, kb), lambda i: (i, 0)),
            pl.BlockSpec((ka, D_MODEL), lambda i: (0, 0)),
            pl.BlockSpec((kb, D_MODEL), lambda i: (1, 0)),
        ],
        out_specs=pl.BlockSpec((tm, D_MODEL), lambda i: (i, 0)),
        out_shape=jax.ShapeDtypeStruct((t, D_MODEL), _F32),
        compiler_params=_params(("parallel",)),
        name="attn_out_proj",
    )(x, oa, ob, w, w)


def _dwconv3(g_ext, cw, tm):
    n = g_ext.shape[0]
    prev = pltpu.roll(g_ext, 1, 0)[HALO:HALO + tm]
    nxt = pltpu.roll(g_ext, n - 1, 0)[HALO:HALO + tm]
    return cw[0:1] * prev + cw[1:2] * g_ext[HALO:HALO + tm] + cw[2:3] * nxt


def _gated_block_kernel(*refs, kind, tm, seq):
    if kind == "ffn":
        x_ref, xp_ref, xn_ref, g_ref, wa_ref, wb_ref, cw_ref, cb_ref, wd_ref, o_ref, hn_ref = refs
    else:
        x_ref, xp_ref, xn_ref, g_ref, wa_ref, wb_ref, wc_ref, cw_ref, wd_ref, o_ref, hn_ref = refs
    i = pl.program_id(0)
    j = pl.program_id(1)
    tiles_per_seq = seq // tm

    @pl.when(j == 0)
    def _():
        g = g_ref[...]
        pos_in_seq = i % tiles_per_seq
        keep_prev = (pos_in_seq != 0).astype(_F32)
        keep_next = (pos_in_seq != tiles_per_seq - 1).astype(_F32)
        hn_ref[0:HALO] = (_rms(xp_ref[...], g) * keep_prev).astype(_BF16)
        hn_ref[HALO:HALO + tm] = _rms(x_ref[...], g).astype(_BF16)
        hn_ref[HALO + tm:] = (_rms(xn_ref[...], g) * keep_next).astype(_BF16)
        o_ref[...] = x_ref[...]

    cw = cw_ref[...]
    sub = min(tm, GATED_SUB)
    for lo in range(0, tm, sub):
        h_ext = hn_ref[lo:lo + sub + 2 * HALO]
        h_main = hn_ref[lo + HALO:lo + HALO + sub]
        if kind == "ffn":
            c = _dwconv3(_dot(h_ext, wa_ref[...]), cw, sub) + cb_ref[...]
            act = c * jax.nn.sigmoid(c) * _dot(h_main, wb_ref[...])
        else:
            act = _dot(h_main, wa_ref[...]) * _dwconv3(_dot(h_ext, wb_ref[...]) * _dot(h_ext, wc_ref[...]), cw, sub)
        o_ref[lo:lo + sub] += _dot(act.astype(_BF16), wd_ref[...])


def _gated_block(kind, x, g, ups, cw, cb, wd, *, seq, tc=512):
    t = x.shape[0]
    tm = GATED_TM[kind]
    c_total = wd.shape[0]
    nc = c_total // tc
    hb = tm // HALO
    row_specs = [
        pl.BlockSpec((tm, D_MODEL), lambda i, j: (i, 0)),
        pl.BlockSpec((HALO, D_MODEL), lambda i, j: (jnp.maximum(i * hb - 1, 0), 0)),
        pl.BlockSpec((HALO, D_MODEL), lambda i, j: (jnp.minimum((i + 1) * hb, t // HALO - 1), 0)),
        pl.BlockSpec((1, D_MODEL), lambda i, j: (0, 0)),
    ]
    if kind == "ffn":
        wg, wu = ups
        args = (x, x, x, g, wg, wu, cw, cb, wd)
        w_specs = [
            pl.BlockSpec((D_MODEL, tc), lambda i, j: (0, j)),
            pl.BlockSpec((D_MODEL, tc), lambda i, j: (0, j)),
            pl.BlockSpec((3, tc), lambda i, j: (0, j)),
            pl.BlockSpec((1, tc), lambda i, j: (0, j)),
        ]
    else:
        (w_in,) = ups
        args = (x, x, x, g, w_in, w_in, w_in, cw, wd)
        w_specs = [
            pl.BlockSpec((D_MODEL, tc), lambda i, j: (0, j)),
            pl.BlockSpec((D_MODEL, tc), lambda i, j: (0, nc + j)),
            pl.BlockSpec((D_MODEL, tc), lambda i, j: (0, 2 * nc + j)),
            pl.BlockSpec((3, tc), lambda i, j: (0, j)),
        ]
    return pl.pallas_call(
        functools.partial(_gated_block_kernel, kind=kind, tm=tm, seq=seq),
        grid=(t // tm, nc),
        in_specs=row_specs + w_specs + [pl.BlockSpec((tc, D_MODEL), lambda i, j: (j, 0))],
        out_specs=pl.BlockSpec((tm, D_MODEL), lambda i, j: (i, 0)),
        out_shape=jax.ShapeDtypeStruct((t, D_MODEL), _F32),
        scratch_shapes=[pltpu.VMEM((tm + 2 * HALO, D_MODEL), _BF16)],
        compiler_params=_params(("parallel", "arbitrary")),
        name=kind + "_block",
    )(*args)


def _pad_lanes(a, width):
    return jnp.pad(a, ((0, 0), (0, width - a.shape[1])))


def _attn_layer(x, pos_col, rel_table, layer_idx, norm_g, w_in_t, w_lat_t, dq_g, dk_g, lq1, lk1, lq2, lk2,
                subln_g, q_a_g, w_uq, kv_a_g, w_ukv, mq_g, mk_g, w_out, *, batch, seq):
    row = lambda a: a.reshape(1, -1)
    lam_init = 0.8 - 0.6 * math.exp(-0.3 * layer_idx)
    inv = 1.0 / (ROPE_THETA ** (jnp.arange(0, B_ROPE_DIM, 2, dtype=_F32) / B_ROPE_DIM))
    hn, rope_tabs = _norm(x, row(norm_g), pos_col, _pad_lanes(jnp.tile(inv, 2).reshape(1, -1), LANES))

    reps = A_W // A_HALF_DIM
    gains = jnp.stack([jnp.tile(dq_g, reps) * (A_HALF_DIM ** -0.5 * LOG2E), jnp.tile(dk_g, reps)])
    qkv = _qkv_proj(hn, w_in_t, layer_idx // 2, gains.reshape(2, 1, A_W))

    w_uq_p = jnp.pad(w_uq.reshape(B_Q_RANK, B_HEADS, B_QK_DIM),
                     ((0, 0), (0, 0), (0, B_QK_PAD - B_QK_DIM))).reshape(B_Q_RANK, -1).astype(_BF16)
    qm, km, vm = _mla_prep(hn, rope_tabs, w_lat_t, row(q_a_g), w_uq_p, row(kv_a_g), w_ukv.astype(_BF16),
                           _pad_lanes(row(mq_g), B_QK_PAD), _pad_lanes(row(mk_g), B_QK_PAD))

    band = _bias_band(rel_table, seq=seq, tq=ATTN_TQ)
    oa = _diff_attn(qkv, band, row(lq1), row(lk1), row(lq2), row(lk2), row(subln_g),
                    batch=batch, seq=seq, lam_init=lam_init, tq=ATTN_TQ)
    ob = _mla_attn(qm, km, vm, batch=batch, seq=seq, tq=ATTN_TQ)
    return _out_proj(x, oa, ob, w_out)


def kernel(x, positions, rel_bias_table, attn_norm_g, attn_w_in, diff_q_norm_g, diff_k_norm_g, diff_lambda_q1, diff_lambda_k1, diff_lambda_q2, diff_lambda_k2, diff_subln_g, mla_q_a_norm_g, mla_w_uq, mla_kv_a_norm_g, mla_w_ukv, mla_q_norm_g, mla_k_norm_g, attn_w_out, conv_norm_g, conv_w_in, conv_w, conv_w_out, ffn_norm_g, ffn_w_gate, ffn_w_up, ffn_dwconv_w, ffn_dwconv_b, ffn_w_down):
    batch, seq, d = x.shape
    depth = ffn_norm_g.shape[0]
    h = x.reshape(batch * seq, d)
    pos_col = positions.reshape(batch * seq, 1)
    for layer in range(depth):
        i = layer // 2
        if layer % 2 == 0:
            w_in_t = jnp.swapaxes(attn_w_in, 1, 2)
            lat_rows = w_in_t.shape[1] - 3 * A_W
            w_lat_t = jnp.pad(w_in_t[i, 3 * A_W:], ((0, LAT_W - lat_rows), (0, 0))).astype(_BF16)
            h = _attn_layer(h, pos_col, rel_bias_table, layer, attn_norm_g[i], w_in_t, w_lat_t, diff_q_norm_g[i],
                            diff_k_norm_g[i], diff_lambda_q1[i], diff_lambda_k1[i], diff_lambda_q2[i],
                            diff_lambda_k2[i], diff_subln_g[i], mla_q_a_norm_g[i], mla_w_uq[i],
                            mla_kv_a_norm_g[i], mla_w_ukv[i], mla_q_norm_g[i], mla_k_norm_g[i],
                            _to_bf16(attn_w_out, i), batch=batch, seq=seq)
        else:
            h = _gated_block("conv", h, conv_norm_g[i].reshape(1, -1), (_to_bf16(conv_w_in, i),),
                             conv_w[i], None, _to_bf16(conv_w_out, i), seq=seq)
        h = _gated_block("ffn", h, ffn_norm_g[layer].reshape(1, -1),
                         (_to_bf16(ffn_w_gate, layer), _to_bf16(ffn_w_up, layer)),
                         ffn_dwconv_w[layer], ffn_dwconv_b[layer].reshape(1, -1),
                         _to_bf16(ffn_w_down, layer), seq=seq)
    return h.reshape(batch, seq, d)
```

```python
import functools
import math

import jax
import jax.numpy as jnp
from jax import lax
from jax.experimental import pallas as pl
from jax.experimental.pallas import tpu as pltpu

D_MODEL = 2048
A_HEADS = 8
A_HALF_DIM = 64
A_V_DIM = 128
B_HEADS = 8
B_NOPE_DIM = 128
B_ROPE_DIM = 64
B_QK_DIM = B_NOPE_DIM + B_ROPE_DIM
B_V_DIM = 128
B_Q_RANK = 512
B_KV_RANK = 256
B_QK_PAD = 256
ROPE_THETA = 10000.0
REL_BUCKETS = 32
REL_MAX_DIST = 128
EPS = 1e-6
A_W = A_HEADS * 2 * A_HALF_DIM
LAT_W = B_Q_RANK + B_KV_RANK + 128

LANES = 128
MXU_DIM = 256
HALO = 16
ATTN_TQ = 256
ATTN_HEAD_GROUP = 2
GATED_TM = {"ffn": 1024, "conv": 512}
GATED_SUB = 512
LOG2E = math.log2(math.e)
VMEM_LIMIT = 62 * 1024 * 1024

_F32 = jnp.float32
_BF16 = jnp.bfloat16


def _rms(xf, g):
    ms = jnp.mean(xf * xf, axis=-1, keepdims=True)
    return xf * lax.rsqrt(ms + EPS) * g


def _dot(a, b):
    return jnp.dot(a, b, preferred_element_type=_F32)


def _dot_nt(a, b):
    return lax.dot_general(a, b, (((1,), (1,)), ((), ())), preferred_element_type=_F32)


def _params(sem):
    return pltpu.CompilerParams(dimension_semantics=sem, vmem_limit_bytes=VMEM_LIMIT)


def _full(a):
    return pl.BlockSpec(a.shape, lambda *_: (0,) * a.ndim)


def _cast_kernel(w_ref, o_ref):
    o_ref[...] = w_ref[...].astype(_BF16)


def _to_bf16(w, layer, *, tr=256):
    _, r, c = w.shape
    return pl.pallas_call(
        _cast_kernel,
        grid=(r // tr,),
        in_specs=[pl.BlockSpec((None, tr, c), lambda i: (layer, i, 0))],
        out_specs=pl.BlockSpec((tr, c), lambda i: (i, 0)),
        out_shape=jax.ShapeDtypeStruct((r, c), _BF16),
        compiler_params=_params(("parallel",)),
        name="weight_to_bf16",
    )(w)


def _norm_kernel(x_ref, g_ref, pos_ref, inv_ref, o_ref, rope_ref):
    o_ref[...] = _rms(x_ref[...], g_ref[...]).astype(_BF16)
    ang = pos_ref[...].astype(_F32) * inv_ref[...]
    lane = lax.broadcasted_iota(jnp.int32, (1, LANES), 1)
    half = B_ROPE_DIM // 2
    cosv = jnp.cos(ang)
    sinv = jnp.sin(ang)
    rope_ref[0] = jnp.where(lane < B_ROPE_DIM, cosv, 0.0)
    rope_ref[1] = jnp.where(lane < half, -sinv, 0.0)
    rope_ref[2] = jnp.where((lane >= half) & (lane < B_ROPE_DIM), sinv, 0.0)


def _norm(x, g, pos, inv, *, tm=1024):
    t = x.shape[0]
    return pl.pallas_call(
        _norm_kernel,
        grid=(t // tm,),
        in_specs=[pl.BlockSpec((tm, D_MODEL), lambda i: (i, 0)), _full(g),
                  pl.BlockSpec((tm, 1), lambda i: (i, 0)), _full(inv)],
        out_specs=[pl.BlockSpec((tm, D_MODEL), lambda i: (i, 0)),
                   pl.BlockSpec((3, tm, LANES), lambda i: (0, i, 0))],
        out_shape=[jax.ShapeDtypeStruct((t, D_MODEL), _BF16), jax.ShapeDtypeStruct((3, t, LANES), _F32)],
        compiler_params=_params(("parallel",)),
        name="pre_norm",
    )(x, g, pos, inv)


def _qkv_proj_kernel(h_ref, w32_ref, g_ref, bd_ref, o_ref, w_ref):
    j = pl.program_id(0)
    h = h_ref[...]

    @pl.when(pl.program_id(1) == 0)
    def _():
        w_ref[...] = w32_ref[...].astype(_BF16)

    @pl.when(j < 2)
    def _():
        bd = bd_ref[...]
        y = _dot_nt(h, w_ref[...])
        y2 = (y * y).astype(_BF16)
        width = bd.shape[0]
        for c in range(w_ref.shape[0] // width):
            cols = slice(c * width, (c + 1) * width)
            ms = _dot(y2[:, cols], bd) * (1.0 / A_HALF_DIM)
            o_ref[:, cols] = (y[:, cols] * lax.rsqrt(ms + EPS) * g_ref[0, :, cols]).astype(_BF16)

    @pl.when(j == 2)
    def _():
        o_ref[...] = _dot_nt(h, w_ref[...]).astype(_BF16)


def _qkv_proj(h, wt, layer, gains, *, tm=512):
    t = h.shape[0]
    grp = jnp.arange(2 * MXU_DIM) // A_HALF_DIM
    bd = (grp[:, None] == grp[None, :]).astype(_BF16)
    return pl.pallas_call(
        _qkv_proj_kernel,
        grid=(3, t // tm),
        in_specs=[
            pl.BlockSpec((tm, D_MODEL), lambda j, i: (i, 0)),
            pl.BlockSpec((None, A_W, D_MODEL), lambda j, i: (layer, j, 0)),
            pl.BlockSpec((1, 1, A_W), lambda j, i: (jnp.minimum(j, 1), 0, 0)),
            _full(bd),
        ],
        out_specs=pl.BlockSpec((tm, A_W), lambda j, i: (i, j)),
        out_shape=jax.ShapeDtypeStruct((t, 3 * A_W), _BF16),
        scratch_shapes=[pltpu.VMEM((A_W, D_MODEL), _BF16)],
        compiler_params=_params(("arbitrary", "arbitrary")),
        name="diff_qkv_proj",
    )(h, wt, gains, bd)


def _mla_prep_kernel(h_ref, rope_ref, wl_ref, qag_ref, wuq_ref, kvag_ref, wukv_ref, mqg_ref, mkg_ref,
                     q_ref, k_ref, v_ref, *, sub):
    half = B_ROPE_DIM // 2
    scale = B_QK_DIM ** -0.5 * LOG2E
    mqg = mqg_ref[...]
    mkg = mkg_ref[...]
    kv_w = B_NOPE_DIM + B_V_DIM

    def project(lo):
        lat = _dot_nt(h_ref[lo:lo + sub], wl_ref[...])
        cq = lat[:, :B_Q_RANK]
        ckv = lat[:, B_Q_RANK:B_Q_RANK + B_KV_RANK]
        kr = lat[:, B_Q_RANK + B_KV_RANK:]
        qf = _dot(_rms(cq, qag_ref[...]).astype(_BF16), wuq_ref[...])
        kv = _dot(_rms(ckv, kvag_ref[...]).astype(_BF16), wukv_ref[...])
        return qf, kv, kr

    def finish(lo, qf, kv, kr):
        rows = slice(lo, lo + sub)
        c_tab, s_lo, s_hi = rope_ref[0, rows], rope_ref[1, rows], rope_ref[2, rows]

        def rope(r):
            return r * c_tab + pltpu.roll(r, LANES - half, 1) * s_lo + pltpu.roll(r, half, 1) * s_hi

        for h in range(B_HEADS):
            slab = qf[:, h * B_QK_PAD:(h + 1) * B_QK_PAD]
            ms = jnp.sum(slab * slab, axis=-1, keepdims=True) * (1.0 / B_QK_DIM)
            sn = slab * (lax.rsqrt(ms + EPS) * scale) * mqg
            q_ref[rows, h * B_QK_PAD:h * B_QK_PAD + B_NOPE_DIM] = sn[:, :B_NOPE_DIM].astype(_BF16)
            q_ref[rows, h * B_QK_PAD + B_NOPE_DIM:(h + 1) * B_QK_PAD] = rope(sn[:, B_NOPE_DIM:]).astype(_BF16)
        kr_ss = jnp.sum(kr * kr, axis=-1, keepdims=True)
        kr_rot = rope(kr * mkg[:, B_NOPE_DIM:])
        for h in range(B_HEADS):
            kn = kv[:, h * kv_w:h * kv_w + B_NOPE_DIM]
            ms = (jnp.sum(kn * kn, axis=-1, keepdims=True) + kr_ss) * (1.0 / B_QK_DIM)
            rs = lax.rsqrt(ms + EPS)
            k_ref[rows, h * B_QK_PAD:h * B_QK_PAD + B_NOPE_DIM] = (kn * rs * mkg[:, :B_NOPE_DIM]).astype(_BF16)
            k_ref[rows, h * B_QK_PAD + B_NOPE_DIM:(h + 1) * B_QK_PAD] = (kr_rot * rs).astype(_BF16)
            v_ref[rows, h * B_V_DIM:(h + 1) * B_V_DIM] = kv[:, h * kv_w + B_NOPE_DIM:(h + 1) * kv_w].astype(_BF16)

    starts = list(range(0, h_ref.shape[0], sub))
    nxt = project(starts[0])
    for n, lo in enumerate(starts):
        cur = nxt
        if n + 1 < len(starts):
            nxt = project(starts[n + 1])
        finish(lo, *cur)


def _mla_prep(h, rope_tabs, wl, qag, wuq, kvag, wukv, mqg, mkg, *, tm=1024, sub=256):
    t = h.shape[0]
    return pl.pallas_call(
        functools.partial(_mla_prep_kernel, sub=sub),
        grid=(t // tm,),
        in_specs=[
            pl.BlockSpec((tm, D_MODEL), lambda i: (i, 0)),
            pl.BlockSpec((3, tm, LANES), lambda i: (0, i, 0)),
            _full(wl), _full(qag), _full(wuq), _full(kvag), _full(wukv), _full(mqg), _full(mkg),
        ],
        out_specs=[
            pl.BlockSpec((tm, B_HEADS * B_QK_PAD), lambda i: (i, 0)),
            pl.BlockSpec((tm, B_HEADS * B_QK_PAD), lambda i: (i, 0)),
            pl.BlockSpec((tm, B_HEADS * B_V_DIM), lambda i: (i, 0)),
        ],
        out_shape=[
            jax.ShapeDtypeStruct((t, B_HEADS * B_QK_PAD), _BF16),
            jax.ShapeDtypeStruct((t, B_HEADS * B_QK_PAD), _BF16),
            jax.ShapeDtypeStruct((t, B_HEADS * B_V_DIM), _BF16),
        ],
        compiler_params=_params(("parallel",)),
        name="mla_prep",
    )(h, rope_tabs, wl, qag, wuq, kvag, wukv, mqg, mkg)


def _t5_bucket_of(rel):
    nb = REL_BUCKETS // 2
    max_exact = nb // 2
    n = jnp.abs(rel)
    large = jnp.full(rel.shape, max_exact, jnp.int32)
    for k in range(1, nb - max_exact):
        large = large + (n >= math.ceil(max_exact * 2.0 ** (k / 2.0))).astype(jnp.int32)
    return jnp.where(rel > 0, nb, 0) + jnp.where(n < max_exact, n, large)


def _bias_of_rel(tbl_ref, head, rel):
    bucket = _t5_bucket_of(rel)
    val = jnp.zeros(rel.shape, _F32)
    for c in range(REL_BUCKETS):
        val = jnp.where(bucket == c, tbl_ref[c * A_HEADS + head], val)
    return val * LOG2E


def _bias_band_kernel(tbl_ref, o_ref, *, seq, tq):
    h = pl.program_id(0)
    blk = REL_MAX_DIST
    row = lax.broadcasted_iota(jnp.int32, (blk, blk), 0)
    col = lax.broadcasted_iota(jnp.int32, (blk, blk), 1)
    tiles = {}
    for d in range(-2, 3):
        rel = d * blk + col - row if abs(d) < 2 else jnp.full((blk, blk), d * blk // 2, jnp.int32)
        tiles[d] = _bias_of_rel(tbl_ref, h, rel)
    shift = (seq - tq) // blk
    for rb in range(tq // blk):
        for cb in range((2 * seq - tq) // blk):
            d = max(-2, min(2, cb - rb - shift))
            o_ref[0, rb * blk:(rb + 1) * blk, cb * blk:(cb + 1) * blk] = tiles[d]


def _bias_band(rel_table, *, seq, tq):
    cols = 2 * seq - tq
    return pl.pallas_call(
        functools.partial(_bias_band_kernel, seq=seq, tq=tq),
        grid=(A_HEADS,),
        in_specs=[pl.BlockSpec(memory_space=pltpu.SMEM)],
        out_specs=pl.BlockSpec((1, tq, cols), lambda h: (h, 0, 0)),
        out_shape=jax.ShapeDtypeStruct((A_HEADS, tq, cols), _F32),
        compiler_params=_params(("arbitrary",)),
        name="rel_bias_band",
    )(rel_table.reshape(-1))


def _softmax_pv(s, va):
    m = jnp.max(s, axis=-1, keepdims=True)
    p = jnp.exp2(s - m).astype(_BF16)
    oa = _dot(p, va)
    dv = va.shape[1] // 2
    return oa[:, :dv] / oa[:, dv:]


def _stage_v(v_ref, va_ref, dv):
    for g in range(va_ref.shape[0]):
        va_ref[g, :, :dv] = v_ref[:, g * dv:(g + 1) * dv]
        va_ref[g, :, dv:] = jnp.ones((v_ref.shape[0], dv), _BF16)


def _diff_attn_kernel(consecutive_ref, tbl_ref, q_ref, k_ref, v_ref, band_ref, qpos_ref, kpos_ref,
                      lq1_ref, lk1_ref, lq2_ref, lk2_ref, sg_ref, o_ref, va_ref, *, tq, seq, lam_init):
    hg = va_ref.shape[0]
    _stage_v(v_ref, va_ref, A_V_DIM)
    lane = lax.broadcasted_iota(jnp.int32, (1, LANES), 1)
    lam = (jnp.exp(jnp.sum(lq1_ref[...] * lk1_ref[...], axis=-1, keepdims=True))
           - jnp.exp(jnp.sum(lq2_ref[...] * lk2_ref[...], axis=-1, keepdims=True)) + lam_init)
    sg = sg_ref[...] * (1.0 - lam_init)

    def qk(g, rows, half):
        cols = slice(g * LANES, (g + 1) * LANES)
        q = q_ref[rows, cols]
        keep = (lane >= A_HALF_DIM) if half else (lane < A_HALF_DIM)
        return _dot_nt(jnp.where(keep, q, jnp.zeros_like(q)), k_ref[:, cols])

    def store(g, rows, o1, o2):
        o_ref[rows, g * A_V_DIM:(g + 1) * A_V_DIM] = _rms(o1 - lam * o2, sg).astype(_BF16)

    consecutive = consecutive_ref[pl.program_id(1)] != 0

    @pl.when(consecutive)
    def _():
        def logits(g, t, half):
            off = seq - (t + 1) * tq
            return qk(g, slice(t * tq, (t + 1) * tq), half) + band_ref[g, :, off:off + seq]

        units = [(g, t, half) for g in range(hg) for t in range(seq // tq) for half in range(2)]
        s_next = logits(*units[0])
        for u, (g, t, half) in enumerate(units):
            s = s_next
            if u + 1 < len(units):
                s_next = logits(*units[u + 1])
            if half == 0:
                o1 = _softmax_pv(s, va_ref[g])
            else:
                store(g, slice(t * tq, (t + 1) * tq), o1, _softmax_pv(s, va_ref[g]))

    @pl.when(jnp.logical_not(consecutive))
    def _():
        k_pos = kpos_ref[...]
        for g in range(hg):
            head = pl.program_id(0) * hg + g

            def tile(t, carry):
                rows = pl.ds(pl.multiple_of(t * tq, tq), tq)
                bias = _bias_of_rel(tbl_ref, head, k_pos - qpos_ref[rows])
                store(g, rows, _softmax_pv(qk(g, rows, 0) + bias, va_ref[g]),
                      _softmax_pv(qk(g, rows, 1) + bias, va_ref[g]))
                return carry

            lax.fori_loop(0, seq // tq, tile, 0)


def _diff_attn(qkv, band, rel_table, positions, lq1, lk1, lq2, lk2, sg, *, seq, lam_init, tq,
               hg=ATTN_HEAD_GROUP):
    batch = positions.shape[0]
    groups = A_HEADS // hg
    w = hg * LANES
    consecutive = jnp.all(positions[:, 1:] - positions[:, :-1] == 1, axis=1).astype(jnp.int32)
    smem = pl.BlockSpec(memory_space=pltpu.SMEM)
    return pl.pallas_call(
        functools.partial(_diff_attn_kernel, tq=tq, seq=seq, lam_init=lam_init),
        grid=(groups, batch),
        in_specs=[
            smem, smem,
            pl.BlockSpec((seq, w), lambda h, b: (b, h)),
            pl.BlockSpec((seq, w), lambda h, b: (b, groups + h)),
            pl.BlockSpec((seq, w), lambda h, b: (b, 2 * groups + h)),
            pl.BlockSpec((hg, tq, band.shape[2]), lambda h, b: (h, 0, 0)),
            pl.BlockSpec((seq, 1), lambda h, b: (b, 0)),
            pl.BlockSpec((None, 1, seq), lambda h, b: (b, 0, 0)),
            _full(lq1), _full(lk1), _full(lq2), _full(lk2), _full(sg),
        ],
        out_specs=pl.BlockSpec((seq, hg * A_V_DIM), lambda h, b: (b, h)),
        out_shape=jax.ShapeDtypeStruct((batch * seq, A_HEADS * A_V_DIM), _BF16),
        scratch_shapes=[pltpu.VMEM((hg, seq, 2 * A_V_DIM), _BF16)],
        compiler_params=_params(("parallel", "arbitrary")),
        name="diff_attn",
    )(consecutive, rel_table.reshape(-1), qkv, qkv, qkv, band, positions.reshape(batch * seq, 1),
      positions.reshape(batch, 1, seq), lq1, lk1, lq2, lk2, sg)


def _mla_attn_kernel(q_ref, k_ref, v_ref, o_ref, va_ref, *, tq, seq):
    _stage_v(v_ref, va_ref, B_V_DIM)

    def logits(g, t):
        cols = slice(g * B_QK_PAD, (g + 1) * B_QK_PAD)
        return _dot_nt(q_ref[t * tq:(t + 1) * tq, cols], k_ref[:, cols])

    units = [(g, t) for g in range(va_ref.shape[0]) for t in range(seq // tq)]
    s_next = logits(*units[0])
    for u, (g, t) in enumerate(units):
        s = s_next
        if u + 1 < len(units):
            s_next = logits(*units[u + 1])
        o_ref[t * tq:(t + 1) * tq, g * B_V_DIM:(g + 1) * B_V_DIM] = _softmax_pv(s, va_ref[g]).astype(_BF16)


def _mla_attn(q, k, v, *, batch, seq, tq, hg=ATTN_HEAD_GROUP):
    return pl.pallas_call(
        functools.partial(_mla_attn_kernel, tq=tq, seq=seq),
        grid=(B_HEADS // hg, batch),
        in_specs=[
            pl.BlockSpec((seq, hg * B_QK_PAD), lambda h, b: (b, h)),
            pl.BlockSpec((seq, hg * B_QK_PAD), lambda h, b: (b, h)),
            pl.BlockSpec((seq, hg * B_V_DIM), lambda h, b: (b, h)),
        ],
        out_specs=pl.BlockSpec((seq, hg * B_V_DIM), lambda h, b: (b, h)),
        out_shape=jax.ShapeDtypeStruct((batch * seq, B_HEADS * B_V_DIM), _BF16),
        scratch_shapes=[pltpu.VMEM((hg, seq, 2 * B_V_DIM), _BF16)],
        compiler_params=_params(("parallel", "arbitrary")),
        name="mla_attn",
    )(q, k, v)


def _out_proj_kernel(x_ref, oa_ref, ob_ref, wa_ref, wb_ref, o_ref):
    o_ref[...] = x_ref[...] + _dot(oa_ref[...], wa_ref[...]) + _dot(ob_ref[...], wb_ref[...])


def _out_proj(x, oa, ob, w, *, tm=512):
    t = x.shape[0]
    ka, kb = oa.shape[1], ob.shape[1]
    assert ka == kb and ka + kb == w.shape[0]
    return pl.pallas_call(
        _out_proj_kernel,
        grid=(t // tm,),
        in_specs=[
            pl.BlockSpec((tm, D_MODEL), lambda i: (i, 0)),
            pl.BlockSpec((tm, ka), lambda i: (i, 0)),
            pl.BlockSpec((tm, kb), lambda i: (i, 0)),
            pl.BlockSpec((ka, D_MODEL), lambda i: (0, 0)),
            pl.BlockSpec((kb, D_MODEL), lambda i: (1, 0)),
        ],
        out_specs=pl.BlockSpec((tm, D_MODEL), lambda i: (i, 0)),
        out_shape=jax.ShapeDtypeStruct((t, D_MODEL), _F32),
        compiler_params=_params(("parallel",)),
        name="attn_out_proj",
    )(x, oa, ob, w, w)


def _dwconv3(g_ext, cw, tm):
    n = g_ext.shape[0]
    prev = pltpu.roll(g_ext, 1, 0)[HALO:HALO + tm]
    nxt = pltpu.roll(g_ext, n - 1, 0)[HALO:HALO + tm]
    return cw[0:1] * prev + cw[1:2] * g_ext[HALO:HALO + tm] + cw[2:3] * nxt


def _gated_block_kernel(*refs, kind, tm, seq):
    if kind == "ffn":
        x_ref, xp_ref, xn_ref, g_ref, wa_ref, wb_ref, cw_ref, cb_ref, wd_ref, o_ref, hn_ref = refs
    else:
        x_ref, xp_ref, xn_ref, g_ref, wa_ref, wb_ref, wc_ref, cw_ref, wd_ref, o_ref, hn_ref = refs
    i = pl.program_id(0)
    j = pl.program_id(1)
    tiles_per_seq = seq // tm

    @pl.when(j == 0)
    def _():
        g = g_ref[...]
        pos_in_seq = i % tiles_per_seq
        keep_prev = (pos_in_seq != 0).astype(_F32)
        keep_next = (pos_in_seq != tiles_per_seq - 1).astype(_F32)
        hn_ref[0:HALO] = (_rms(xp_ref[...], g) * keep_prev).astype(_BF16)
        hn_ref[HALO:HALO + tm] = _rms(x_ref[...], g).astype(_BF16)
        hn_ref[HALO + tm:] = (_rms(xn_ref[...], g) * keep_next).astype(_BF16)
        o_ref[...] = x_ref[...]

    cw = cw_ref[...]
    sub = min(tm, GATED_SUB)
    for lo in range(0, tm, sub):
        h_ext = hn_ref[lo:lo + sub + 2 * HALO]
        h_main = hn_ref[lo + HALO:lo + HALO + sub]
        if kind == "ffn":
            c = _dwconv3(_dot(h_ext, wa_ref[...]), cw, sub) + cb_ref[...]
            act = c * jax.nn.sigmoid(c) * _dot(h_main, wb_ref[...])
        else:
            act = _dot(h_main, wa_ref[...]) * _dwconv3(_dot(h_ext, wb_ref[...]) * _dot(h_ext, wc_ref[...]), cw, sub)
        o_ref[lo:lo + sub] += _dot(act.astype(_BF16), wd_ref[...])


def _gated_block(kind, x, g, ups, cw, cb, wd, *, seq, tc=512):
    t = x.shape[0]
    tm = GATED_TM[kind]
    c_total = wd.shape[0]
    nc = c_total // tc
    hb = tm // HALO
    row_specs = [
        pl.BlockSpec((tm, D_MODEL), lambda i, j: (i, 0)),
        pl.BlockSpec((HALO, D_MODEL), lambda i, j: (jnp.maximum(i * hb - 1, 0), 0)),
        pl.BlockSpec((HALO, D_MODEL), lambda i, j: (jnp.minimum((i + 1) * hb, t // HALO - 1), 0)),
        pl.BlockSpec((1, D_MODEL), lambda i, j: (0, 0)),
    ]
    if kind == "ffn":
        wg, wu = ups
        args = (x, x, x, g, wg, wu, cw, cb, wd)
        w_specs = [
            pl.BlockSpec((D_MODEL, tc), lambda i, j: (0, j)),
            pl.BlockSpec((D_MODEL, tc), lambda i, j: (0, j)),
            pl.BlockSpec((3, tc), lambda i, j: (0, j)),
            pl.BlockSpec((1, tc), lambda i, j: (0, j)),
        ]
    else:
        (w_in,) = ups
        args = (x, x, x, g, w_in, w_in, w_in, cw, wd)
        w_specs = [
            pl.BlockSpec((D_MODEL, tc), lambda i, j: (0, j)),
            pl.BlockSpec((D_MODEL, tc), lambda i, j: (0, nc + j)),
            pl.BlockSpec((D_MODEL, tc), lambda i, j: (0, 2 * nc + j)),
            pl.BlockSpec((3, tc), lambda i, j: (0, j)),
        ]
    return pl.pallas_call(
        functools.partial(_gated_block_kernel, kind=kind, tm=tm, seq=seq),
        grid=(t // tm, nc),
        in_specs=row_specs + w_specs + [pl.BlockSpec((tc, D_MODEL), lambda i, j: (j, 0))],
        out_specs=pl.BlockSpec((tm, D_MODEL), lambda i, j: (i, 0)),
        out_shape=jax.ShapeDtypeStruct((t, D_MODEL), _F32),
        scratch_shapes=[pltpu.VMEM((tm + 2 * HALO, D_MODEL), _BF16)],
        compiler_params=_params(("parallel", "arbitrary")),
        name=kind + "_block",
    )(*args)


def _pad_lanes(a, width):
    return jnp.pad(a, ((0, 0), (0, width - a.shape[1])))


def _attn_layer(x, positions, rel_table, layer_idx, norm_g, w_in_t, w_lat_t, dq_g, dk_g, lq1, lk1, lq2, lk2,
                subln_g, q_a_g, w_uq, kv_a_g, w_ukv, mq_g, mk_g, w_out, *, batch, seq):
    row = lambda a: a.reshape(1, -1)
    lam_init = 0.8 - 0.6 * math.exp(-0.3 * layer_idx)
    inv = 1.0 / (ROPE_THETA ** (jnp.arange(0, B_ROPE_DIM, 2, dtype=_F32) / B_ROPE_DIM))
    hn, rope_tabs = _norm(x, row(norm_g), positions.reshape(-1, 1), _pad_lanes(jnp.tile(inv, 2).reshape(1, -1), LANES))

    reps = A_W // A_HALF_DIM
    gains = jnp.stack([jnp.tile(dq_g, reps) * (A_HALF_DIM ** -0.5 * LOG2E), jnp.tile(dk_g, reps)])
    qkv = _qkv_proj(hn, w_in_t, layer_idx // 2, gains.reshape(2, 1, A_W))

    w_uq_p = jnp.pad(w_uq.reshape(B_Q_RANK, B_HEADS, B_QK_DIM),
                     ((0, 0), (0, 0), (0, B_QK_PAD - B_QK_DIM))).reshape(B_Q_RANK, -1).astype(_BF16)
    qm, km, vm = _mla_prep(hn, rope_tabs, w_lat_t, row(q_a_g), w_uq_p, row(kv_a_g), w_ukv.astype(_BF16),
                           _pad_lanes(row(mq_g), B_QK_PAD), _pad_lanes(row(mk_g), B_QK_PAD))

    band = _bias_band(rel_table, seq=seq, tq=ATTN_TQ)
    oa = _diff_attn(qkv, band, rel_table, positions, row(lq1), row(lk1), row(lq2), row(lk2), row(subln_g),
                    seq=seq, lam_init=lam_init, tq=ATTN_TQ)
    ob = _mla_attn(qm, km, vm, batch=batch, seq=seq, tq=ATTN_TQ)
    return _out_proj(x, oa, ob, w_out)


def kernel(x, positions, rel_bias_table, attn_norm_g, attn_w_in, diff_q_norm_g, diff_k_norm_g, diff_lambda_q1, diff_lambda_k1, diff_lambda_q2, diff_lambda_k2, diff_subln_g, mla_q_a_norm_g, mla_w_uq, mla_kv_a_norm_g, mla_w_ukv, mla_q_norm_g, mla_k_norm_g, attn_w_out, conv_norm_g, conv_w_in, conv_w, conv_w_out, ffn_norm_g, ffn_w_gate, ffn_w_up, ffn_dwconv_w, ffn_dwconv_b, ffn_w_down):
    batch, seq, d = x.shape
    depth = ffn_norm_g.shape[0]
    h = x.reshape(batch * seq, d)
    for layer in range(depth):
        i = layer // 2
        if layer % 2 == 0:
            w_in_t = jnp.swapaxes(attn_w_in, 1, 2)
            lat_rows = w_in_t.shape[1] - 3 * A_W
            w_lat_t = jnp.pad(w_in_t[i, 3 * A_W:], ((0, LAT_W - lat_rows), (0, 0))).astype(_BF16)
            h = _attn_layer(h, positions, rel_bias_table, layer, attn_norm_g[i], w_in_t, w_lat_t, diff_q_norm_g[i],
                            diff_k_norm_g[i], diff_lambda_q1[i], diff_lambda_k1[i], diff_lambda_q2[i],
                            diff_lambda_k2[i], diff_subln_g[i], mla_q_a_norm_g[i], mla_w_uq[i],
                            mla_kv_a_norm_g[i], mla_w_ukv[i], mla_q_norm_g[i], mla_k_norm_g[i],
                            _to_bf16(attn_w_out, i), batch=batch, seq=seq)
        else:
            h = _gated_block("conv", h, conv_norm_g[i].reshape(1, -1), (_to_bf16(conv_w_in, i),),
                             conv_w[i], None, _to_bf16(conv_w_out, i), seq=seq)
        h = _gated_block("ffn", h, ffn_norm_g[layer].reshape(1, -1),
                         (_to_bf16(ffn_w_gate, layer), _to_bf16(ffn_w_up, layer)),
                         ffn_dwconv_w[layer], ffn_dwconv_b[layer].reshape(1, -1),
                         _to_bf16(ffn_w_down, layer), seq=seq)
    return h.reshape(batch, seq, d)
```

```python
import functools
import math

import jax
import jax.numpy as jnp
from jax import lax
from jax.experimental import pallas as pl
from jax.experimental.pallas import tpu as pltpu

D_MODEL = 2048
A_HEADS = 8
A_HALF_DIM = 64
A_V_DIM = 128
B_HEADS = 8
B_NOPE_DIM = 128
B_ROPE_DIM = 64
B_QK_DIM = B_NOPE_DIM + B_ROPE_DIM
B_V_DIM = 128
B_Q_RANK = 512
B_KV_RANK = 256
B_QK_PAD = 256
ROPE_THETA = 10000.0
REL_BUCKETS = 32
REL_MAX_DIST = 128
EPS = 1e-6
A_W = A_HEADS * 2 * A_HALF_DIM
LAT_W = B_Q_RANK + B_KV_RANK + 128

LANES = 128
MXU_DIM = 256
HALO = 16
ATTN_TQ = 256
ATTN_HEAD_GROUP = 2
GATED_TM = {"ffn": 1024, "conv": 512}
GATED_SUB = 512
LOG2E = math.log2(math.e)
VMEM_LIMIT = 62 * 1024 * 1024

_F32 = jnp.float32
_BF16 = jnp.bfloat16


def _rms(xf, g):
    ms = jnp.mean(xf * xf, axis=-1, keepdims=True)
    return xf * lax.rsqrt(ms + EPS) * g


def _dot(a, b):
    return jnp.dot(a, b, preferred_element_type=_F32)


def _dot_nt(a, b):
    return lax.dot_general(a, b, (((1,), (1,)), ((), ())), preferred_element_type=_F32)


def _params(sem):
    return pltpu.CompilerParams(dimension_semantics=sem, vmem_limit_bytes=VMEM_LIMIT)


def _full(a):
    return pl.BlockSpec(a.shape, lambda *_: (0,) * a.ndim)


def _cast_kernel(w_ref, o_ref):
    o_ref[...] = w_ref[...].astype(_BF16)


def _to_bf16(w, layer, *, tr=256):
    _, r, c = w.shape
    return pl.pallas_call(
        _cast_kernel,
        grid=(r // tr,),
        in_specs=[pl.BlockSpec((None, tr, c), lambda i: (layer, i, 0))],
        out_specs=pl.BlockSpec((tr, c), lambda i: (i, 0)),
        out_shape=jax.ShapeDtypeStruct((r, c), _BF16),
        compiler_params=_params(("parallel",)),
        name="weight_to_bf16",
    )(w)


def _norm_kernel(x_ref, g_ref, pos_ref, inv_ref, o_ref, rope_ref):
    o_ref[...] = _rms(x_ref[...], g_ref[...]).astype(_BF16)
    ang = pos_ref[...].astype(_F32) * inv_ref[...]
    lane = lax.broadcasted_iota(jnp.int32, (1, LANES), 1)
    half = B_ROPE_DIM // 2
    cosv = jnp.cos(ang)
    sinv = jnp.sin(ang)
    rope_ref[0] = jnp.where(lane < B_ROPE_DIM, cosv, 0.0)
    rope_ref[1] = jnp.where(lane < half, -sinv, 0.0)
    rope_ref[2] = jnp.where((lane >= half) & (lane < B_ROPE_DIM), sinv, 0.0)


def _norm(x, g, pos, inv, *, tm=1024):
    t = x.shape[0]
    return pl.pallas_call(
        _norm_kernel,
        grid=(t // tm,),
        in_specs=[pl.BlockSpec((tm, D_MODEL), lambda i: (i, 0)), _full(g),
                  pl.BlockSpec((tm, 1), lambda i: (i, 0)), _full(inv)],
        out_specs=[pl.BlockSpec((tm, D_MODEL), lambda i: (i, 0)),
                   pl.BlockSpec((3, tm, LANES), lambda i: (0, i, 0))],
        out_shape=[jax.ShapeDtypeStruct((t, D_MODEL), _BF16), jax.ShapeDtypeStruct((3, t, LANES), _F32)],
        compiler_params=_params(("parallel",)),
        name="pre_norm",
    )(x, g, pos, inv)


def _qkv_proj_kernel(h_ref, w32_ref, g_ref, bd_ref, o_ref, w_ref):
    j = pl.program_id(0)
    h = h_ref[...]

    @pl.when(pl.program_id(1) == 0)
    def _():
        w_ref[...] = w32_ref[...].astype(_BF16)

    @pl.when(j < 2)
    def _():
        bd = bd_ref[...]
        y = _dot_nt(h, w_ref[...])
        y2 = (y * y).astype(_BF16)
        width = bd.shape[0]
        for c in range(w_ref.shape[0] // width):
            cols = slice(c * width, (c + 1) * width)
            ms = _dot(y2[:, cols], bd) * (1.0 / A_HALF_DIM)
            o_ref[:, cols] = (y[:, cols] * lax.rsqrt(ms + EPS) * g_ref[0, :, cols]).astype(_BF16)

    @pl.when(j == 2)
    def _():
        o_ref[...] = _dot_nt(h, w_ref[...]).astype(_BF16)


def _qkv_proj(h, wt, layer, gains, *, tm=512):
    t = h.shape[0]
    grp = jnp.arange(2 * MXU_DIM) // A_HALF_DIM
    bd = (grp[:, None] == grp[None, :]).astype(_BF16)
    return pl.pallas_call(
        _qkv_proj_kernel,
        grid=(3, t // tm),
        in_specs=[
            pl.BlockSpec((tm, D_MODEL), lambda j, i: (i, 0)),
            pl.BlockSpec((None, A_W, D_MODEL), lambda j, i: (layer, j, 0)),
            pl.BlockSpec((1, 1, A_W), lambda j, i: (jnp.minimum(j, 1), 0, 0)),
            _full(bd),
        ],
        out_specs=pl.BlockSpec((tm, A_W), lambda j, i: (i, j)),
        out_shape=jax.ShapeDtypeStruct((t, 3 * A_W), _BF16),
        scratch_shapes=[pltpu.VMEM((A_W, D_MODEL), _BF16)],
        compiler_params=_params(("arbitrary", "arbitrary")),
        name="diff_qkv_proj",
    )(h, wt, gains, bd)


def _mla_prep_kernel(h_ref, rope_ref, wl_ref, qag_ref, wuq_ref, kvag_ref, wukv_ref, mqg_ref, mkg_ref,
                     q_ref, k_ref, v_ref, *, sub):
    half = B_ROPE_DIM // 2
    scale = B_QK_DIM ** -0.5 * LOG2E
    mqg = mqg_ref[...]
    mkg = mkg_ref[...]
    kv_w = B_NOPE_DIM + B_V_DIM

    def project(lo):
        lat = _dot_nt(h_ref[lo:lo + sub], wl_ref[...])
        cq = lat[:, :B_Q_RANK]
        ckv = lat[:, B_Q_RANK:B_Q_RANK + B_KV_RANK]
        kr = lat[:, B_Q_RANK + B_KV_RANK:]
        qf = _dot(_rms(cq, qag_ref[...]).astype(_BF16), wuq_ref[...])
        kv = _dot(_rms(ckv, kvag_ref[...]).astype(_BF16), wukv_ref[...])
        return qf, kv, kr

    def finish(lo, qf, kv, kr):
        rows = slice(lo, lo + sub)
        c_tab, s_lo, s_hi = rope_ref[0, rows], rope_ref[1, rows], rope_ref[2, rows]

        def rope(r):
            return r * c_tab + pltpu.roll(r, LANES - half, 1) * s_lo + pltpu.roll(r, half, 1) * s_hi

        for h in range(B_HEADS):
            slab = qf[:, h * B_QK_PAD:(h + 1) * B_QK_PAD]
            ms = jnp.sum(slab * slab, axis=-1, keepdims=True) * (1.0 / B_QK_DIM)
            sn = slab * (lax.rsqrt(ms + EPS) * scale) * mqg
            q_ref[rows, h * B_QK_PAD:h * B_QK_PAD + B_NOPE_DIM] = sn[:, :B_NOPE_DIM].astype(_BF16)
            q_ref[rows, h * B_QK_PAD + B_NOPE_DIM:(h + 1) * B_QK_PAD] = rope(sn[:, B_NOPE_DIM:]).astype(_BF16)
        kr_ss = jnp.sum(kr * kr, axis=-1, keepdims=True)
        kr_rot = rope(kr * mkg[:, B_NOPE_DIM:])
        for h in range(B_HEADS):
            kn = kv[:, h * kv_w:h * kv_w + B_NOPE_DIM]
            ms = (jnp.sum(kn * kn, axis=-1, keepdims=True) + kr_ss) * (1.0 / B_QK_DIM)
            rs = lax.rsqrt(ms + EPS)
            k_ref[rows, h * B_QK_PAD:h * B_QK_PAD + B_NOPE_DIM] = (kn * rs * mkg[:, :B_NOPE_DIM]).astype(_BF16)
            k_ref[rows, h * B_QK_PAD + B_NOPE_DIM:(h + 1) * B_QK_PAD] = (kr_rot * rs).astype(_BF16)
            v_ref[rows, h * B_V_DIM:(h + 1) * B_V_DIM] = kv[:, h * kv_w + B_NOPE_DIM:(h + 1) * kv_w].astype(_BF16)

    starts = list(range(0, h_ref.shape[0], sub))
    nxt = project(starts[0])
    for n, lo in enumerate(starts):
        cur = nxt
        if n + 1 < len(starts):
            nxt = project(starts[n + 1])
        finish(lo, *cur)


def _mla_prep(h, rope_tabs, wl, qag, wuq, kvag, wukv, mqg, mkg, *, tm=1024, sub=256):
    t = h.shape[0]
    return pl.pallas_call(
        functools.partial(_mla_prep_kernel, sub=sub),
        grid=(t // tm,),
        in_specs=[
            pl.BlockSpec((tm, D_MODEL), lambda i: (i, 0)),
            pl.BlockSpec((3, tm, LANES), lambda i: (0, i, 0)),
            _full(wl), _full(qag), _full(wuq), _full(kvag), _full(wukv), _full(mqg), _full(mkg),
        ],
        out_specs=[
            pl.BlockSpec((tm, B_HEADS * B_QK_PAD), lambda i: (i, 0)),
            pl.BlockSpec((tm, B_HEADS * B_QK_PAD), lambda i: (i, 0)),
            pl.BlockSpec((tm, B_HEADS * B_V_DIM), lambda i: (i, 0)),
        ],
        out_shape=[
            jax.ShapeDtypeStruct((t, B_HEADS * B_QK_PAD), _BF16),
            jax.ShapeDtypeStruct((t, B_HEADS * B_QK_PAD), _BF16),
            jax.ShapeDtypeStruct((t, B_HEADS * B_V_DIM), _BF16),
        ],
        compiler_params=_params(("parallel",)),
        name="mla_prep",
    )(h, rope_tabs, wl, qag, wuq, kvag, wukv, mqg, mkg)


def _t5_bucket_of(rel):
    nb = REL_BUCKETS // 2
    max_exact = nb // 2
    n = jnp.abs(rel)
    large = jnp.full(rel.shape, max_exact, jnp.int32)
    for k in range(1, nb - max_exact):
        large = large + (n >= math.ceil(max_exact * 2.0 ** (k / 2.0))).astype(jnp.int32)
    return jnp.where(rel > 0, nb, 0) + jnp.where(n < max_exact, n, large)


def _bias_of_rel(tbl_ref, head, rel):
    bucket = _t5_bucket_of(rel)
    val = jnp.zeros(rel.shape, _F32)
    for c in range(REL_BUCKETS):
        val = jnp.where(bucket == c, tbl_ref[c * A_HEADS + head], val)
    return val * LOG2E


def _bias_band_kernel(tbl_ref, o_ref, *, seq, tq):
    h = pl.program_id(0)
    blk = REL_MAX_DIST
    row = lax.broadcasted_iota(jnp.int32, (blk, blk), 0)
    col = lax.broadcasted_iota(jnp.int32, (blk, blk), 1)
    tiles = {}
    for d in range(-2, 3):
        rel = d * blk + col - row if abs(d) < 2 else jnp.full((blk, blk), d * blk // 2, jnp.int32)
        tiles[d] = _bias_of_rel(tbl_ref, h, rel)
    shift = (seq - tq) // blk
    for rb in range(tq // blk):
        for cb in range((2 * seq - tq) // blk):
            d = max(-2, min(2, cb - rb - shift))
            o_ref[0, rb * blk:(rb + 1) * blk, cb * blk:(cb + 1) * blk] = tiles[d]


def _bias_band(rel_table, *, seq, tq):
    cols = 2 * seq - tq
    return pl.pallas_call(
        functools.partial(_bias_band_kernel, seq=seq, tq=tq),
        grid=(A_HEADS,),
        in_specs=[pl.BlockSpec(memory_space=pltpu.SMEM)],
        out_specs=pl.BlockSpec((1, tq, cols), lambda h: (h, 0, 0)),
        out_shape=jax.ShapeDtypeStruct((A_HEADS, tq, cols), _F32),
        compiler_params=_params(("arbitrary",)),
        name="rel_bias_band",
    )(rel_table.reshape(-1))


def _softmax_pv(s, va):
    m = jnp.max(s, axis=-1, keepdims=True)
    p = jnp.exp2(s - m).astype(_BF16)
    oa = _dot(p, va)
    dv = va.shape[1] // 2
    return oa[:, :dv] / oa[:, dv:]


def _stage_v(v_ref, va_ref, dv):
    for g in range(va_ref.shape[0]):
        va_ref[g, :, :dv] = v_ref[:, g * dv:(g + 1) * dv]
        va_ref[g, :, dv:] = jnp.ones((v_ref.shape[0], dv), _BF16)


def _diff_attn_kernel(tbl_ref, q_ref, k_ref, v_ref, bias_ref, kpos_ref, lq1_ref, lk1_ref, lq2_ref, lk2_ref,
                      sg_ref, o_ref, va_ref, *, tq, seq, lam_init, from_band):
    hg = va_ref.shape[0]
    _stage_v(v_ref, va_ref, A_V_DIM)
    lane = lax.broadcasted_iota(jnp.int32, (1, LANES), 1)
    lam = (jnp.exp(jnp.sum(lq1_ref[...] * lk1_ref[...], axis=-1, keepdims=True))
           - jnp.exp(jnp.sum(lq2_ref[...] * lk2_ref[...], axis=-1, keepdims=True)) + lam_init)
    sg = sg_ref[...] * (1.0 - lam_init)

    def qk(g, rows, half):
        cols = slice(g * LANES, (g + 1) * LANES)
        q = q_ref[rows, cols]
        keep = (lane >= A_HALF_DIM) if half else (lane < A_HALF_DIM)
        return _dot_nt(jnp.where(keep, q, jnp.zeros_like(q)), k_ref[:, cols])

    def store(g, rows, o1, o2):
        o_ref[rows, g * A_V_DIM:(g + 1) * A_V_DIM] = _rms(o1 - lam * o2, sg).astype(_BF16)

    if from_band:
        def logits(g, t, half):
            off = seq - (t + 1) * tq
            return qk(g, slice(t * tq, (t + 1) * tq), half) + bias_ref[g, :, off:off + seq]

        units = [(g, t, half) for g in range(hg) for t in range(seq // tq) for half in range(2)]
        s_next = logits(*units[0])
        for u, (g, t, half) in enumerate(units):
            s = s_next
            if u + 1 < len(units):
                s_next = logits(*units[u + 1])
            if half == 0:
                o1 = _softmax_pv(s, va_ref[g])
            else:
                store(g, slice(t * tq, (t + 1) * tq), o1, _softmax_pv(s, va_ref[g]))
    else:
        k_pos = kpos_ref[...]
        for g in range(hg):
            head = pl.program_id(0) * hg + g

            def tile(t, carry):
                rows = pl.ds(pl.multiple_of(t * tq, tq), tq)
                bias = _bias_of_rel(tbl_ref, head, k_pos - bias_ref[rows])
                store(g, rows, _softmax_pv(qk(g, rows, 0) + bias, va_ref[g]),
                      _softmax_pv(qk(g, rows, 1) + bias, va_ref[g]))
                return carry

            lax.fori_loop(0, seq // tq, tile, 0)


def _diff_attn(qkv, rel_table, positions, lq1, lk1, lq2, lk2, sg, *, seq, lam_init, tq, hg=ATTN_HEAD_GROUP):
    batch = positions.shape[0]
    groups = A_HEADS // hg
    w = hg * LANES

    def call(from_band, bias, bias_spec):
        return pl.pallas_call(
            functools.partial(_diff_attn_kernel, tq=tq, seq=seq, lam_init=lam_init, from_band=from_band),
            grid=(groups, batch),
            in_specs=[
                pl.BlockSpec(memory_space=pltpu.SMEM),
                pl.BlockSpec((seq, w), lambda h, b: (b, h)),
                pl.BlockSpec((seq, w), lambda h, b: (b, groups + h)),
                pl.BlockSpec((seq, w), lambda h, b: (b, 2 * groups + h)),
                bias_spec,
                pl.BlockSpec((None, 1, seq), lambda h, b: (b, 0, 0)),
                _full(lq1), _full(lk1), _full(lq2), _full(lk2), _full(sg),
            ],
            out_specs=pl.BlockSpec((seq, hg * A_V_DIM), lambda h, b: (b, h)),
            out_shape=jax.ShapeDtypeStruct((batch * seq, A_HEADS * A_V_DIM), _BF16),
            scratch_shapes=[pltpu.VMEM((hg, seq, 2 * A_V_DIM), _BF16)],
            compiler_params=_params(("parallel", "arbitrary")),
            name="diff_attn_band" if from_band else "diff_attn_general",
        )(rel_table.reshape(-1), qkv, qkv, qkv, bias, positions.reshape(batch, 1, seq), lq1, lk1, lq2, lk2, sg)

    def banded():
        band = _bias_band(rel_table, seq=seq, tq=tq)
        return call(True, band, pl.BlockSpec((hg, tq, band.shape[2]), lambda h, b: (h, 0, 0)))

    def general():
        return call(False, positions.reshape(batch * seq, 1), pl.BlockSpec((seq, 1), lambda h, b: (b, 0)))

    consecutive = jnp.all(positions[:, 1:] - positions[:, :-1] == 1)
    return lax.cond(consecutive, banded, general)


def _mla_attn_kernel(q_ref, k_ref, v_ref, o_ref, va_ref, *, tq, seq):
    _stage_v(v_ref, va_ref, B_V_DIM)

    def logits(g, t):
        cols = slice(g * B_QK_PAD, (g + 1) * B_QK_PAD)
        return _dot_nt(q_ref[t * tq:(t + 1) * tq, cols], k_ref[:, cols])

    units = [(g, t) for g in range(va_ref.shape[0]) for t in range(seq // tq)]
    s_next = logits(*units[0])
    for u, (g, t) in enumerate(units):
        s = s_next
        if u + 1 < len(units):
            s_next = logits(*units[u + 1])
        o_ref[t * tq:(t + 1) * tq, g * B_V_DIM:(g + 1) * B_V_DIM] = _softmax_pv(s, va_ref[g]).astype(_BF16)


def _mla_attn(q, k, v, *, batch, seq, tq, hg=ATTN_HEAD_GROUP):
    return pl.pallas_call(
        functools.partial(_mla_attn_kernel, tq=tq, seq=seq),
        grid=(B_HEADS // hg, batch),
        in_specs=[
            pl.BlockSpec((seq, hg * B_QK_PAD), lambda h, b: (b, h)),
            pl.BlockSpec((seq, hg * B_QK_PAD), lambda h, b: (b, h)),
            pl.BlockSpec((seq, hg * B_V_DIM), lambda h, b: (b, h)),
        ],
        out_specs=pl.BlockSpec((seq, hg * B_V_DIM), lambda h, b: (b, h)),
        out_shape=jax.ShapeDtypeStruct((batch * seq, B_HEADS * B_V_DIM), _BF16),
        scratch_shapes=[pltpu.VMEM((hg, seq, 2 * B_V_DIM), _BF16)],
        compiler_params=_params(("parallel", "arbitrary")),
        name="mla_attn",
    )(q, k, v)


def _out_proj_kernel(x_ref, oa_ref, ob_ref, wa_ref, wb_ref, o_ref):
    o_ref[...] = x_ref[...] + _dot(oa_ref[...], wa_ref[...]) + _dot(ob_ref[...], wb_ref[...])


def _out_proj(x, oa, ob, w, *, tm=512):
    t = x.shape[0]
    ka, kb = oa.shape[1], ob.shape[1]
    assert ka == kb and ka + kb == w.shape[0]
    return pl.pallas_call(
        _out_proj_kernel,
        grid=(t // tm,),
        in_specs=[
            pl.BlockSpec((tm, D_MODEL), lambda i: (i, 0)),
            pl.BlockSpec((tm, ka), lambda i: (i, 0)),
            pl.BlockSpec((tm, kb), lambda i: (i, 0)),
            pl.BlockSpec((ka, D_MODEL), lambda i: (0, 0)),
            pl.BlockSpec((kb, D_MODEL), lambda i: (1, 0)),
        ],
        out_specs=pl.BlockSpec((tm, D_MODEL), lambda i: (i, 0)),
        out_shape=jax.ShapeDtypeStruct((t, D_MODEL), _F32),
        compiler_params=_params(("parallel",)),
        name="attn_out_proj",
    )(x, oa, ob, w, w)


def _dwconv3(g_ext, cw, tm):
    n = g_ext.shape[0]
    prev = pltpu.roll(g_ext, 1, 0)[HALO:HALO + tm]
    nxt = pltpu.roll(g_ext, n - 1, 0)[HALO:HALO + tm]
    return cw[0:1] * prev + cw[1:2] * g_ext[HALO:HALO + tm] + cw[2:3] * nxt


def _gated_block_kernel(*refs, kind, tm, seq):
    if kind == "ffn":
        x_ref, xp_ref, xn_ref, g_ref, wa_ref, wb_ref, cw_ref, cb_ref, wd_ref, o_ref, hn_ref = refs
    else:
        x_ref, xp_ref, xn_ref, g_ref, wa_ref, wb_ref, wc_ref, cw_ref, wd_ref, o_ref, hn_ref = refs
    i = pl.program_id(0)
    j = pl.program_id(1)
    tiles_per_seq = seq // tm

    @pl.when(j == 0)
    def _():
        g = g_ref[...]
        pos_in_seq = i % tiles_per_seq
        keep_prev = (pos_in_seq != 0).astype(_F32)
        keep_next = (pos_in_seq != tiles_per_seq - 1).astype(_F32)
        hn_ref[0:HALO] = (_rms(xp_ref[...], g) * keep_prev).astype(_BF16)
        hn_ref[HALO:HALO + tm] = _rms(x_ref[...], g).astype(_BF16)
        hn_ref[HALO + tm:] = (_rms(xn_ref[...], g) * keep_next).astype(_BF16)
        o_ref[...] = x_ref[...]

    cw = cw_ref[...]
    sub = min(tm, GATED_SUB)
    for lo in range(0, tm, sub):
        h_ext = hn_ref[lo:lo + sub + 2 * HALO]
        h_main = hn_ref[lo + HALO:lo + HALO + sub]
        if kind == "ffn":
            c = _dwconv3(_dot(h_ext, wa_ref[...]), cw, sub) + cb_ref[...]
            act = c * jax.nn.sigmoid(c) * _dot(h_main, wb_ref[...])
        else:
            act = _dot(h_main, wa_ref[...]) * _dwconv3(_dot(h_ext, wb_ref[...]) * _dot(h_ext, wc_ref[...]), cw, sub)
        o_ref[lo:lo + sub] += _dot(act.astype(_BF16), wd_ref[...])


def _gated_block(kind, x, g, ups, cw, cb, wd, *, seq, tc=512):
    t = x.shape[0]
    tm = GATED_TM[kind]
    c_total = wd.shape[0]
    nc = c_total // tc
    hb = tm // HALO
    row_specs = [
        pl.BlockSpec((tm, D_MODEL), lambda i, j: (i, 0)),
        pl.BlockSpec((HALO, D_MODEL), lambda i, j: (jnp.maximum(i * hb - 1, 0), 0)),
        pl.BlockSpec((HALO, D_MODEL), lambda i, j: (jnp.minimum((i + 1) * hb, t // HALO - 1), 0)),
        pl.BlockSpec((1, D_MODEL), lambda i, j: (0, 0)),
    ]
    if kind == "ffn":
        wg, wu = ups
        args = (x, x, x, g, wg, wu, cw, cb, wd)
        w_specs = [
            pl.BlockSpec((D_MODEL, tc), lambda i, j: (0, j)),
            pl.BlockSpec((D_MODEL, tc), lambda i, j: (0, j)),
            pl.BlockSpec((3, tc), lambda i, j: (0, j)),
            pl.BlockSpec((1, tc), lambda i, j: (0, j)),
        ]
    else:
        (w_in,) = ups
        args = (x, x, x, g, w_in, w_in, w_in, cw, wd)
        w_specs = [
            pl.BlockSpec((D_MODEL, tc), lambda i, j: (0, j)),
            pl.BlockSpec((D_MODEL, tc), lambda i, j: (0, nc + j)),
            pl.BlockSpec((D_MODEL, tc), lambda i, j: (0, 2 * nc + j)),
            pl.BlockSpec((3, tc), lambda i, j: (0, j)),
        ]
    return pl.pallas_call(
        functools.partial(_gated_block_kernel, kind=kind, tm=tm, seq=seq),
        grid=(t // tm, nc),
        in_specs=row_specs + w_specs + [pl.BlockSpec((tc, D_MODEL), lambda i, j: (j, 0))],
        out_specs=pl.BlockSpec((tm, D_MODEL), lambda i, j: (i, 0)),
        out_shape=jax.ShapeDtypeStruct((t, D_MODEL), _F32),
        scratch_shapes=[pltpu.VMEM((tm + 2 * HALO, D_MODEL), _BF16)],
        compiler_params=_params(("parallel", "arbitrary")),
        name=kind + "_block",
    )(*args)


def _pad_lanes(a, width):
    return jnp.pad(a, ((0, 0), (0, width - a.shape[1])))


def _attn_layer(x, positions, rel_table, layer_idx, norm_g, w_in_t, w_lat_t, dq_g, dk_g, lq1, lk1, lq2, lk2,
                subln_g, q_a_g, w_uq, kv_a_g, w_ukv, mq_g, mk_g, w_out, *, batch, seq):
    row = lambda a: a.reshape(1, -1)
    lam_init = 0.8 - 0.6 * math.exp(-0.3 * layer_idx)
    inv = 1.0 / (ROPE_THETA ** (jnp.arange(0, B_ROPE_DIM, 2, dtype=_F32) / B_ROPE_DIM))
    hn, rope_tabs = _norm(x, row(norm_g), positions.reshape(-1, 1), _pad_lanes(jnp.tile(inv, 2).reshape(1, -1), LANES))

    reps = A_W // A_HALF_DIM
    gains = jnp.stack([jnp.tile(dq_g, reps) * (A_HALF_DIM ** -0.5 * LOG2E), jnp.tile(dk_g, reps)])
    qkv = _qkv_proj(hn, w_in_t, layer_idx // 2, gains.reshape(2, 1, A_W))

    w_uq_p = jnp.pad(w_uq.reshape(B_Q_RANK, B_HEADS, B_QK_DIM),
                     ((0, 0), (0, 0), (0, B_QK_PAD - B_QK_DIM))).reshape(B_Q_RANK, -1).astype(_BF16)
    qm, km, vm = _mla_prep(hn, rope_tabs, w_lat_t, row(q_a_g), w_uq_p, row(kv_a_g), w_ukv.astype(_BF16),
                           _pad_lanes(row(mq_g), B_QK_PAD), _pad_lanes(row(mk_g), B_QK_PAD))

    oa = _diff_attn(qkv, rel_table, positions, row(lq1), row(lk1), row(lq2), row(lk2), row(subln_g),
                    seq=seq, lam_init=lam_init, tq=ATTN_TQ)
    ob = _mla_attn(qm, km, vm, batch=batch, seq=seq, tq=ATTN_TQ)
    return _out_proj(x, oa, ob, w_out)


def kernel(x, positions, rel_bias_table, attn_norm_g, attn_w_in, diff_q_norm_g, diff_k_norm_g, diff_lambda_q1, diff_lambda_k1, diff_lambda_q2, diff_lambda_k2, diff_subln_g, mla_q_a_norm_g, mla_w_uq, mla_kv_a_norm_g, mla_w_ukv, mla_q_norm_g, mla_k_norm_g, attn_w_out, conv_norm_g, conv_w_in, conv_w, conv_w_out, ffn_norm_g, ffn_w_gate, ffn_w_up, ffn_dwconv_w, ffn_dwconv_b, ffn_w_down):
    batch, seq, d = x.shape
    depth = ffn_norm_g.shape[0]
    h = x.reshape(batch * seq, d)
    for layer in range(depth):
        i = layer // 2
        if layer % 2 == 0:
            w_in_t = jnp.swapaxes(attn_w_in, 1, 2)
            lat_rows = w_in_t.shape[1] - 3 * A_W
            w_lat_t = jnp.pad(w_in_t[i, 3 * A_W:], ((0, LAT_W - lat_rows), (0, 0))).astype(_BF16)
            h = _attn_layer(h, positions, rel_bias_table, layer, attn_norm_g[i], w_in_t, w_lat_t, diff_q_norm_g[i],
                            diff_k_norm_g[i], diff_lambda_q1[i], diff_lambda_k1[i], diff_lambda_q2[i],
                            diff_lambda_k2[i], diff_subln_g[i], mla_q_a_norm_g[i], mla_w_uq[i],
                            mla_kv_a_norm_g[i], mla_w_ukv[i], mla_q_norm_g[i], mla_k_norm_g[i],
                            _to_bf16(attn_w_out, i), batch=batch, seq=seq)
        else:
            h = _gated_block("conv", h, conv_norm_g[i].reshape(1, -1), (_to_bf16(conv_w_in, i),),
                             conv_w[i], None, _to_bf16(conv_w_out, i), seq=seq)
        h = _gated_block("ffn", h, ffn_norm_g[layer].reshape(1, -1),
                         (_to_bf16(ffn_w_gate, layer), _to_bf16(ffn_w_up, layer)),
                         ffn_dwconv_w[layer], ffn_dwconv_b[layer].reshape(1, -1),
                         _to_bf16(ffn_w_down, layer), seq=seq)
    return h.reshape(batch, seq, d)
```

```python
import functools
import math

import jax
import jax.numpy as jnp
from jax import lax
from jax.experimental import pallas as pl
from jax.experimental.pallas import tpu as pltpu

D_MODEL = 2048
A_HEADS = 8
A_HALF_DIM = 64
A_V_DIM = 128
B_HEADS = 8
B_NOPE_DIM = 128
B_ROPE_DIM = 64
B_QK_DIM = B_NOPE_DIM + B_ROPE_DIM
B_V_DIM = 128
B_Q_RANK = 512
B_KV_RANK = 256
B_QK_PAD = 256
ROPE_THETA = 10000.0
REL_BUCKETS = 32
REL_MAX_DIST = 128
EPS = 1e-6
A_W = A_HEADS * 2 * A_HALF_DIM
LAT_W = B_Q_RANK + B_KV_RANK + 128

LANES = 128
MXU_DIM = 256
HALO = 16
ATTN_TQ = 256
ATTN_HEAD_GROUP = 2
GATED_TM = 1024
GATED_SUB = 512
LOG2E = math.log2(math.e)
VMEM_LIMIT = 62 * 1024 * 1024

_F32 = jnp.float32
_BF16 = jnp.bfloat16


def _rms(xf, g):
    ms = jnp.mean(xf * xf, axis=-1, keepdims=True)
    return xf * lax.rsqrt(ms + EPS) * g


def _dot(a, b):
    return jnp.dot(a, b, preferred_element_type=_F32)


def _dot_nt(a, b):
    return lax.dot_general(a, b, (((1,), (1,)), ((), ())), preferred_element_type=_F32)


def _params(sem):
    return pltpu.CompilerParams(dimension_semantics=sem, vmem_limit_bytes=VMEM_LIMIT)


def _full(a):
    return pl.BlockSpec(a.shape, lambda *_: (0,) * a.ndim)


def _cast_kernel(w_ref, o_ref):
    o_ref[...] = w_ref[...].astype(_BF16)


def _to_bf16(w, layer, *, tr=256):
    _, r, c = w.shape
    return pl.pallas_call(
        _cast_kernel,
        grid=(r // tr,),
        in_specs=[pl.BlockSpec((None, tr, c), lambda i: (layer, i, 0))],
        out_specs=pl.BlockSpec((tr, c), lambda i: (i, 0)),
        out_shape=jax.ShapeDtypeStruct((r, c), _BF16),
        compiler_params=_params(("parallel",)),
        name="weight_to_bf16",
    )(w)


def _norm_kernel(x_ref, g_ref, pos_ref, inv_ref, o_ref, rope_ref):
    o_ref[...] = _rms(x_ref[...], g_ref[...]).astype(_BF16)
    ang = pos_ref[...].astype(_F32) * inv_ref[...]
    lane = lax.broadcasted_iota(jnp.int32, (1, LANES), 1)
    half = B_ROPE_DIM // 2
    cosv = jnp.cos(ang)
    sinv = jnp.sin(ang)
    rope_ref[0] = jnp.where(lane < B_ROPE_DIM, cosv, 0.0)
    rope_ref[1] = jnp.where(lane < half, -sinv, 0.0)
    rope_ref[2] = jnp.where((lane >= half) & (lane < B_ROPE_DIM), sinv, 0.0)


def _norm(x, g, pos, inv, *, tm=1024):
    t = x.shape[0]
    return pl.pallas_call(
        _norm_kernel,
        grid=(t // tm,),
        in_specs=[pl.BlockSpec((tm, D_MODEL), lambda i: (i, 0)), _full(g),
                  pl.BlockSpec((tm, 1), lambda i: (i, 0)), _full(inv)],
        out_specs=[pl.BlockSpec((tm, D_MODEL), lambda i: (i, 0)),
                   pl.BlockSpec((3, tm, LANES), lambda i: (0, i, 0))],
        out_shape=[jax.ShapeDtypeStruct((t, D_MODEL), _BF16), jax.ShapeDtypeStruct((3, t, LANES), _F32)],
        compiler_params=_params(("parallel",)),
        name="pre_norm",
    )(x, g, pos, inv)


def _qkv_proj_kernel(h_ref, w32_ref, g_ref, bd_ref, o_ref, w_ref):
    j = pl.program_id(0)
    h = h_ref[...]

    @pl.when(pl.program_id(1) == 0)
    def _():
        w_ref[...] = w32_ref[...].astype(_BF16)

    @pl.when(j < 2)
    def _():
        bd = bd_ref[...]
        y = _dot_nt(h, w_ref[...])
        y2 = (y * y).astype(_BF16)
        width = bd.shape[0]
        for c in range(w_ref.shape[0] // width):
            cols = slice(c * width, (c + 1) * width)
            ms = _dot(y2[:, cols], bd) * (1.0 / A_HALF_DIM)
            o_ref[:, cols] = (y[:, cols] * lax.rsqrt(ms + EPS) * g_ref[0, :, cols]).astype(_BF16)

    @pl.when(j == 2)
    def _():
        o_ref[...] = _dot_nt(h, w_ref[...]).astype(_BF16)


def _qkv_proj(h, wt, layer, gains, *, tm=1024):
    t = h.shape[0]
    grp = jnp.arange(2 * MXU_DIM) // A_HALF_DIM
    bd = (grp[:, None] == grp[None, :]).astype(_BF16)
    return pl.pallas_call(
        _qkv_proj_kernel,
        grid=(3, t // tm),
        in_specs=[
            pl.BlockSpec((tm, D_MODEL), lambda j, i: (i, 0)),
            pl.BlockSpec((None, A_W, D_MODEL), lambda j, i: (layer, j, 0)),
            pl.BlockSpec((1, 1, A_W), lambda j, i: (jnp.minimum(j, 1), 0, 0)),
            _full(bd),
        ],
        out_specs=pl.BlockSpec((tm, A_W), lambda j, i: (i, j)),
        out_shape=jax.ShapeDtypeStruct((t, 3 * A_W), _BF16),
        scratch_shapes=[pltpu.VMEM((A_W, D_MODEL), _BF16)],
        compiler_params=_params(("arbitrary", "arbitrary")),
        name="diff_qkv_proj",
    )(h, wt, gains, bd)


def _mla_prep_kernel(h_ref, rope_ref, wl_ref, qag_ref, wuq_ref, kvag_ref, wukv_ref, mqg_ref, mkg_ref,
                     q_ref, k_ref, v_ref, *, sub):
    half = B_ROPE_DIM // 2
    scale = B_QK_DIM ** -0.5 * LOG2E
    mqg = mqg_ref[...]
    mkg = mkg_ref[...]
    kv_w = B_NOPE_DIM + B_V_DIM

    def project(lo):
        lat = _dot_nt(h_ref[lo:lo + sub], wl_ref[...])
        cq = lat[:, :B_Q_RANK]
        ckv = lat[:, B_Q_RANK:B_Q_RANK + B_KV_RANK]
        kr = lat[:, B_Q_RANK + B_KV_RANK:]
        qf = _dot(_rms(cq, qag_ref[...]).astype(_BF16), wuq_ref[...])
        kv = _dot(_rms(ckv, kvag_ref[...]).astype(_BF16), wukv_ref[...])
        return qf, kv, kr

    def finish(lo, qf, kv, kr):
        rows = slice(lo, lo + sub)
        c_tab, s_lo, s_hi = rope_ref[0, rows], rope_ref[1, rows], rope_ref[2, rows]

        def rope(r):
            return r * c_tab + pltpu.roll(r, LANES - half, 1) * s_lo + pltpu.roll(r, half, 1) * s_hi

        for h in range(B_HEADS):
            slab = qf[:, h * B_QK_PAD:(h + 1) * B_QK_PAD]
            ms = jnp.sum(slab * slab, axis=-1, keepdims=True) * (1.0 / B_QK_DIM)
            sn = slab * (lax.rsqrt(ms + EPS) * scale) * mqg
            q_ref[rows, h * B_QK_PAD:h * B_QK_PAD + B_NOPE_DIM] = sn[:, :B_NOPE_DIM].astype(_BF16)
            q_ref[rows, h * B_QK_PAD + B_NOPE_DIM:(h + 1) * B_QK_PAD] = rope(sn[:, B_NOPE_DIM:]).astype(_BF16)
        kr_ss = jnp.sum(kr * kr, axis=-1, keepdims=True)
        kr_rot = rope(kr * mkg[:, B_NOPE_DIM:])
        for h in range(B_HEADS):
            kn = kv[:, h * kv_w:h * kv_w + B_NOPE_DIM]
            ms = (jnp.sum(kn * kn, axis=-1, keepdims=True) + kr_ss) * (1.0 / B_QK_DIM)
            rs = lax.rsqrt(ms + EPS)
            k_ref[rows, h * B_QK_PAD:h * B_QK_PAD + B_NOPE_DIM] = (kn * rs * mkg[:, :B_NOPE_DIM]).astype(_BF16)
            k_ref[rows, h * B_QK_PAD + B_NOPE_DIM:(h + 1) * B_QK_PAD] = (kr_rot * rs).astype(_BF16)
            v_ref[rows, h * B_V_DIM:(h + 1) * B_V_DIM] = kv[:, h * kv_w + B_NOPE_DIM:(h + 1) * kv_w].astype(_BF16)

    starts = list(range(0, h_ref.shape[0], sub))
    nxt = project(starts[0])
    for n, lo in enumerate(starts):
        cur = nxt
        if n + 1 < len(starts):
            nxt = project(starts[n + 1])
        finish(lo, *cur)


def _mla_prep(h, rope_tabs, wl, qag, wuq, kvag, wukv, mqg, mkg, *, tm=1024, sub=256):
    t = h.shape[0]
    return pl.pallas_call(
        functools.partial(_mla_prep_kernel, sub=sub),
        grid=(t // tm,),
        in_specs=[
            pl.BlockSpec((tm, D_MODEL), lambda i: (i, 0)),
            pl.BlockSpec((3, tm, LANES), lambda i: (0, i, 0)),
            _full(wl), _full(qag), _full(wuq), _full(kvag), _full(wukv), _full(mqg), _full(mkg),
        ],
        out_specs=[
            pl.BlockSpec((tm, B_HEADS * B_QK_PAD), lambda i: (i, 0)),
            pl.BlockSpec((tm, B_HEADS * B_QK_PAD), lambda i: (i, 0)),
            pl.BlockSpec((tm, B_HEADS * B_V_DIM), lambda i: (i, 0)),
        ],
        out_shape=[
            jax.ShapeDtypeStruct((t, B_HEADS * B_QK_PAD), _BF16),
            jax.ShapeDtypeStruct((t, B_HEADS * B_QK_PAD), _BF16),
            jax.ShapeDtypeStruct((t, B_HEADS * B_V_DIM), _BF16),
        ],
        compiler_params=_params(("parallel",)),
        name="mla_prep",
    )(h, rope_tabs, wl, qag, wuq, kvag, wukv, mqg, mkg)


def _t5_bucket_of(rel):
    nb = REL_BUCKETS // 2
    max_exact = nb // 2
    n = jnp.abs(rel)
    large = jnp.full(rel.shape, max_exact, jnp.int32)
    for k in range(1, nb - max_exact):
        large = large + (n >= math.ceil(max_exact * 2.0 ** (k / 2.0))).astype(jnp.int32)
    return jnp.where(rel > 0, nb, 0) + jnp.where(n < max_exact, n, large)


def _bias_of_rel(tbl_ref, head, rel):
    bucket = _t5_bucket_of(rel)
    val = jnp.zeros(rel.shape, _F32)
    for c in range(REL_BUCKETS):
        val = jnp.where(bucket == c, tbl_ref[c * A_HEADS + head], val)
    return val * LOG2E


def _bias_band_kernel(tbl_ref, o_ref, *, seq, tq):
    h = pl.program_id(0)
    blk = REL_MAX_DIST
    row = lax.broadcasted_iota(jnp.int32, (blk, blk), 0)
    col = lax.broadcasted_iota(jnp.int32, (blk, blk), 1)
    tiles = {}
    for d in range(-2, 3):
        rel = d * blk + col - row if abs(d) < 2 else jnp.full((blk, blk), d * blk // 2, jnp.int32)
        tiles[d] = _bias_of_rel(tbl_ref, h, rel)
    shift = (seq - tq) // blk
    for rb in range(tq // blk):
        for cb in range((2 * seq - tq) // blk):
            d = max(-2, min(2, cb - rb - shift))
            o_ref[0, rb * blk:(rb + 1) * blk, cb * blk:(cb + 1) * blk] = tiles[d]


def _bias_band(rel_table, *, seq, tq):
    cols = 2 * seq - tq
    return pl.pallas_call(
        functools.partial(_bias_band_kernel, seq=seq, tq=tq),
        grid=(A_HEADS,),
        in_specs=[pl.BlockSpec(memory_space=pltpu.SMEM)],
        out_specs=pl.BlockSpec((1, tq, cols), lambda h: (h, 0, 0)),
        out_shape=jax.ShapeDtypeStruct((A_HEADS, tq, cols), _F32),
        compiler_params=_params(("arbitrary",)),
        name="rel_bias_band",
    )(rel_table.reshape(-1))


def _softmax_pv(s, va):
    m = jnp.max(s, axis=-1, keepdims=True)
    p = jnp.exp2(s - m).astype(_BF16)
    oa = _dot(p, va)
    dv = va.shape[1] // 2
    return oa[:, :dv] / oa[:, dv:]


def _stage_v(v_ref, va_ref, dv):
    for g in range(va_ref.shape[0]):
        va_ref[g, :, :dv] = v_ref[:, g * dv:(g + 1) * dv]
        va_ref[g, :, dv:] = jnp.ones((v_ref.shape[0], dv), _BF16)


def _diff_attn_kernel(tbl_ref, q_ref, k_ref, v_ref, bias_ref, kpos_ref, lq1_ref, lk1_ref, lq2_ref, lk2_ref,
                      sg_ref, o_ref, va_ref, *, tq, seq, lam_init, from_band):
    hg = va_ref.shape[0]
    _stage_v(v_ref, va_ref, A_V_DIM)
    lane = lax.broadcasted_iota(jnp.int32, (1, LANES), 1)
    lam = (jnp.exp(jnp.sum(lq1_ref[...] * lk1_ref[...], axis=-1, keepdims=True))
           - jnp.exp(jnp.sum(lq2_ref[...] * lk2_ref[...], axis=-1, keepdims=True)) + lam_init)
    sg = sg_ref[...] * (1.0 - lam_init)

    def qk(g, rows, half):
        cols = slice(g * LANES, (g + 1) * LANES)
        q = q_ref[rows, cols]
        keep = (lane >= A_HALF_DIM) if half else (lane < A_HALF_DIM)
        return _dot_nt(jnp.where(keep, q, jnp.zeros_like(q)), k_ref[:, cols])

    def store(g, rows, o1, o2):
        o_ref[rows, g * A_V_DIM:(g + 1) * A_V_DIM] = _rms(o1 - lam * o2, sg).astype(_BF16)

    if from_band:
        def logits(g, t, half):
            off = seq - (t + 1) * tq
            return qk(g, slice(t * tq, (t + 1) * tq), half) + bias_ref[g, :, off:off + seq]

        units = [(g, t, half) for g in range(hg) for t in range(seq // tq) for half in range(2)]
        s_next = logits(*units[0])
        for u, (g, t, half) in enumerate(units):
            s = s_next
            if u + 1 < len(units):
                s_next = logits(*units[u + 1])
            if half == 0:
                o1 = _softmax_pv(s, va_ref[g])
            else:
                store(g, slice(t * tq, (t + 1) * tq), o1, _softmax_pv(s, va_ref[g]))
    else:
        k_pos = kpos_ref[...]
        for g in range(hg):
            head = pl.program_id(0) * hg + g

            def tile(t, carry):
                rows = pl.ds(pl.multiple_of(t * tq, tq), tq)
                bias = _bias_of_rel(tbl_ref, head, k_pos - bias_ref[rows])
                store(g, rows, _softmax_pv(qk(g, rows, 0) + bias, va_ref[g]),
                      _softmax_pv(qk(g, rows, 1) + bias, va_ref[g]))
                return carry

            lax.fori_loop(0, seq // tq, tile, 0)


def _diff_attn(qkv, rel_table, positions, lq1, lk1, lq2, lk2, sg, *, seq, lam_init, tq, hg=ATTN_HEAD_GROUP):
    batch = positions.shape[0]
    groups = A_HEADS // hg
    w = hg * LANES

    def call(from_band, bias, bias_spec):
        return pl.pallas_call(
            functools.partial(_diff_attn_kernel, tq=tq, seq=seq, lam_init=lam_init, from_band=from_band),
            grid=(groups, batch),
            in_specs=[
                pl.BlockSpec(memory_space=pltpu.SMEM),
                pl.BlockSpec((seq, w), lambda h, b: (b, h)),
                pl.BlockSpec((seq, w), lambda h, b: (b, groups + h)),
                pl.BlockSpec((seq, w), lambda h, b: (b, 2 * groups + h)),
                bias_spec,
                pl.BlockSpec((None, 1, seq), lambda h, b: (b, 0, 0)),
                _full(lq1), _full(lk1), _full(lq2), _full(lk2), _full(sg),
            ],
            out_specs=pl.BlockSpec((seq, hg * A_V_DIM), lambda h, b: (b, h)),
            out_shape=jax.ShapeDtypeStruct((batch * seq, A_HEADS * A_V_DIM), _BF16),
            scratch_shapes=[pltpu.VMEM((hg, seq, 2 * A_V_DIM), _BF16)],
            compiler_params=_params(("parallel", "arbitrary")),
            name="diff_attn_band" if from_band else "diff_attn_general",
        )(rel_table.reshape(-1), qkv, qkv, qkv, bias, positions.reshape(batch, 1, seq), lq1, lk1, lq2, lk2, sg)

    def banded():
        band = _bias_band(rel_table, seq=seq, tq=tq)
        return call(True, band, pl.BlockSpec((hg, tq, band.shape[2]), lambda h, b: (h, 0, 0)))

    def general():
        return call(False, positions.reshape(batch * seq, 1), pl.BlockSpec((seq, 1), lambda h, b: (b, 0)))

    consecutive = jnp.all(positions[:, 1:] - positions[:, :-1] == 1)
    return lax.cond(consecutive, banded, general)


def _mla_attn_kernel(q_ref, k_ref, v_ref, o_ref, va_ref, *, tq, seq):
    _stage_v(v_ref, va_ref, B_V_DIM)

    def logits(g, t):
        cols = slice(g * B_QK_PAD, (g + 1) * B_QK_PAD)
        return _dot_nt(q_ref[t * tq:(t + 1) * tq, cols], k_ref[:, cols])

    units = [(g, t) for g in range(va_ref.shape[0]) for t in range(seq // tq)]
    s_next = logits(*units[0])
    for u, (g, t) in enumerate(units):
        s = s_next
        if u + 1 < len(units):
            s_next = logits(*units[u + 1])
        o_ref[t * tq:(t + 1) * tq, g * B_V_DIM:(g + 1) * B_V_DIM] = _softmax_pv(s, va_ref[g]).astype(_BF16)


def _mla_attn(q, k, v, *, batch, seq, tq, hg=ATTN_HEAD_GROUP):
    return pl.pallas_call(
        functools.partial(_mla_attn_kernel, tq=tq, seq=seq),
        grid=(B_HEADS // hg, batch),
        in_specs=[
            pl.BlockSpec((seq, hg * B_QK_PAD), lambda h, b: (b, h)),
            pl.BlockSpec((seq, hg * B_QK_PAD), lambda h, b: (b, h)),
            pl.BlockSpec((seq, hg * B_V_DIM), lambda h, b: (b, h)),
        ],
        out_specs=pl.BlockSpec((seq, hg * B_V_DIM), lambda h, b: (b, h)),
        out_shape=jax.ShapeDtypeStruct((batch * seq, B_HEADS * B_V_DIM), _BF16),
        scratch_shapes=[pltpu.VMEM((hg, seq, 2 * B_V_DIM), _BF16)],
        compiler_params=_params(("parallel", "arbitrary")),
        name="mla_attn",
    )(q, k, v)


def _out_proj_kernel(x_ref, oa_ref, ob_ref, wa_ref, wb_ref, o_ref):
    o_ref[...] = x_ref[...] + _dot(oa_ref[...], wa_ref[...]) + _dot(ob_ref[...], wb_ref[...])


def _out_proj(x, oa, ob, w, *, tm=512):
    t = x.shape[0]
    ka, kb = oa.shape[1], ob.shape[1]
    assert ka == kb and ka + kb == w.shape[0]
    return pl.pallas_call(
        _out_proj_kernel,
        grid=(t // tm,),
        in_specs=[
            pl.BlockSpec((tm, D_MODEL), lambda i: (i, 0)),
            pl.BlockSpec((tm, ka), lambda i: (i, 0)),
            pl.BlockSpec((tm, kb), lambda i: (i, 0)),
            pl.BlockSpec((ka, D_MODEL), lambda i: (0, 0)),
            pl.BlockSpec((kb, D_MODEL), lambda i: (1, 0)),
        ],
        out_specs=pl.BlockSpec((tm, D_MODEL), lambda i: (i, 0)),
        out_shape=jax.ShapeDtypeStruct((t, D_MODEL), _F32),
        compiler_params=_params(("parallel",)),
        name="attn_out_proj",
    )(x, oa, ob, w, w)


def _dwconv3(g_ext, cw, tm):
    n = g_ext.shape[0]
    prev = pltpu.roll(g_ext, 1, 0)[HALO:HALO + tm]
    nxt = pltpu.roll(g_ext, n - 1, 0)[HALO:HALO + tm]
    return cw[0:1] * prev + cw[1:2] * g_ext[HALO:HALO + tm] + cw[2:3] * nxt


def _gated_block_kernel(*refs, kind, tm, seq):
    if kind == "ffn":
        x_hbm, xp_ref, xn_ref, g_ref, wa_ref, wb_ref, cw_ref, cb_ref, wd_ref, o_ref, hn_ref, x_ref, sem = refs
    else:
        x_hbm, xp_ref, xn_ref, g_ref, wa_ref, wb_ref, wc_ref, cw_ref, wd_ref, o_ref, hn_ref, x_ref, sem = refs
    i = pl.program_id(0)
    j = pl.program_id(1)
    tiles_per_seq = seq // tm

    def x_copy(tile):
        return pltpu.make_async_copy(x_hbm.at[pl.ds(tile * tm, tm)], x_ref, sem)

    @pl.when((i == 0) & (j == 0))
    def _():
        x_copy(0).start()

    @pl.when(j == 0)
    def _():
        x_copy(i).wait()
        g = g_ref[...]
        pos_in_seq = i % tiles_per_seq
        keep_prev = (pos_in_seq != 0).astype(_F32)
        keep_next = (pos_in_seq != tiles_per_seq - 1).astype(_F32)
        hn_ref[0:HALO] = (_rms(xp_ref[...], g) * keep_prev).astype(_BF16)
        hn_ref[HALO:HALO + tm] = _rms(x_ref[...], g).astype(_BF16)
        hn_ref[HALO + tm:] = (_rms(xn_ref[...], g) * keep_next).astype(_BF16)
        o_ref[...] = x_ref[...]

    @pl.when((j == 1) & (i + 1 < pl.num_programs(0)))
    def _():
        x_copy(i + 1).start()

    cw = cw_ref[...]
    sub = min(tm, GATED_SUB)
    for lo in range(0, tm, sub):
        h_ext = hn_ref[lo:lo + sub + 2 * HALO]
        h_main = hn_ref[lo + HALO:lo + HALO + sub]
        if kind == "ffn":
            c = _dwconv3(_dot(h_ext, wa_ref[...]), cw, sub) + cb_ref[...]
            act = c * jax.nn.sigmoid(c) * _dot(h_main, wb_ref[...])
        else:
            act = _dot(h_main, wa_ref[...]) * _dwconv3(_dot(h_ext, wb_ref[...]) * _dot(h_ext, wc_ref[...]), cw, sub)
        o_ref[lo:lo + sub] += _dot(act.astype(_BF16), wd_ref[...])


def _gated_block(kind, x, g, ups, cw, cb, wd, *, seq, tc=512):
    t = x.shape[0]
    tm = GATED_TM
    c_total = wd.shape[0]
    nc = c_total // tc
    hb = tm // HALO
    row_specs = [
        pl.BlockSpec(memory_space=pl.ANY),
        pl.BlockSpec((HALO, D_MODEL), lambda i, j: (jnp.maximum(i * hb - 1, 0), 0)),
        pl.BlockSpec((HALO, D_MODEL), lambda i, j: (jnp.minimum((i + 1) * hb, t // HALO - 1), 0)),
        pl.BlockSpec((1, D_MODEL), lambda i, j: (0, 0)),
    ]
    if kind == "ffn":
        wg, wu = ups
        args = (x, x, x, g, wg, wu, cw, cb, wd)
        w_specs = [
            pl.BlockSpec((D_MODEL, tc), lambda i, j: (0, j)),
            pl.BlockSpec((D_MODEL, tc), lambda i, j: (0, j)),
            pl.BlockSpec((3, tc), lambda i, j: (0, j)),
            pl.BlockSpec((1, tc), lambda i, j: (0, j)),
        ]
    else:
        (w_in,) = ups
        args = (x, x, x, g, w_in, w_in, w_in, cw, wd)
        w_specs = [
            pl.BlockSpec((D_MODEL, tc), lambda i, j: (0, j)),
            pl.BlockSpec((D_MODEL, tc), lambda i, j: (0, nc + j)),
            pl.BlockSpec((D_MODEL, tc), lambda i, j: (0, 2 * nc + j)),
            pl.BlockSpec((3, tc), lambda i, j: (0, j)),
        ]
    return pl.pallas_call(
        functools.partial(_gated_block_kernel, kind=kind, tm=tm, seq=seq),
        grid=(t // tm, nc),
        in_specs=row_specs + w_specs + [pl.BlockSpec((tc, D_MODEL), lambda i, j: (j, 0))],
        out_specs=pl.BlockSpec((tm, D_MODEL), lambda i, j: (i, 0)),
        out_shape=jax.ShapeDtypeStruct((t, D_MODEL), _F32),
        scratch_shapes=[pltpu.VMEM((tm + 2 * HALO, D_MODEL), _BF16), pltpu.VMEM((tm, D_MODEL), _F32),
                        pltpu.SemaphoreType.DMA(())],
        compiler_params=_params(("arbitrary", "arbitrary")),
        name=kind + "_block",
    )(*args)


def _pad_lanes(a, width):
    return jnp.pad(a, ((0, 0), (0, width - a.shape[1])))


def _attn_layer(x, positions, rel_table, layer_idx, norm_g, w_in_t, w_lat_t, dq_g, dk_g, lq1, lk1, lq2, lk2,
                subln_g, q_a_g, w_uq, kv_a_g, w_ukv, mq_g, mk_g, w_out, *, batch, seq):
    row = lambda a: a.reshape(1, -1)
    lam_init = 0.8 - 0.6 * math.exp(-0.3 * layer_idx)
    inv = 1.0 / (ROPE_THETA ** (jnp.arange(0, B_ROPE_DIM, 2, dtype=_F32) / B_ROPE_DIM))
    hn, rope_tabs = _norm(x, row(norm_g), positions.reshape(-1, 1), _pad_lanes(jnp.tile(inv, 2).reshape(1, -1), LANES))

    reps = A_W // A_HALF_DIM
    gains = jnp.stack([jnp.tile(dq_g, reps) * (A_HALF_DIM ** -0.5 * LOG2E), jnp.tile(dk_g, reps)])
    qkv = _qkv_proj(hn, w_in_t, layer_idx // 2, gains.reshape(2, 1, A_W))

    w_uq_p = jnp.pad(w_uq.reshape(B_Q_RANK, B_HEADS, B_QK_DIM),
                     ((0, 0), (0, 0), (0, B_QK_PAD - B_QK_DIM))).reshape(B_Q_RANK, -1).astype(_BF16)
    qm, km, vm = _mla_prep(hn, rope_tabs, w_lat_t, row(q_a_g), w_uq_p, row(kv_a_g), w_ukv.astype(_BF16),
                           _pad_lanes(row(mq_g), B_QK_PAD), _pad_lanes(row(mk_g), B_QK_PAD))

    oa = _diff_attn(qkv, rel_table, positions, row(lq1), row(lk1), row(lq2), row(lk2), row(subln_g),
                    seq=seq, lam_init=lam_init, tq=ATTN_TQ)
    ob = _mla_attn(qm, km, vm, batch=batch, seq=seq, tq=ATTN_TQ)
    return _out_proj(x, oa, ob, w_out)


def kernel(x, positions, rel_bias_table, attn_norm_g, attn_w_in, diff_q_norm_g, diff_k_norm_g, diff_lambda_q1, diff_lambda_k1, diff_lambda_q2, diff_lambda_k2, diff_subln_g, mla_q_a_norm_g, mla_w_uq, mla_kv_a_norm_g, mla_w_ukv, mla_q_norm_g, mla_k_norm_g, attn_w_out, conv_norm_g, conv_w_in, conv_w, conv_w_out, ffn_norm_g, ffn_w_gate, ffn_w_up, ffn_dwconv_w, ffn_dwconv_b, ffn_w_down):
    batch, seq, d = x.shape
    depth = ffn_norm_g.shape[0]
    h = x.reshape(batch * seq, d)
    for layer in range(depth):
        i = layer // 2
        if layer % 2 == 0:
            w_in_t = jnp.swapaxes(attn_w_in, 1, 2)
            lat_rows = w_in_t.shape[1] - 3 * A_W
            w_lat_t = jnp.pad(w_in_t[i, 3 * A_W:], ((0, LAT_W - lat_rows), (0, 0))).astype(_BF16)
            h = _attn_layer(h, positions, rel_bias_table, layer, attn_norm_g[i], w_in_t, w_lat_t, diff_q_norm_g[i],
                            diff_k_norm_g[i], diff_lambda_q1[i], diff_lambda_k1[i], diff_lambda_q2[i],
                            diff_lambda_k2[i], diff_subln_g[i], mla_q_a_norm_g[i], mla_w_uq[i],
                            mla_kv_a_norm_g[i], mla_w_ukv[i], mla_q_norm_g[i], mla_k_norm_g[i],
                            _to_bf16(attn_w_out, i), batch=batch, seq=seq)
        else:
            h = _gated_block("conv", h, conv_norm_g[i].reshape(1, -1), (_to_bf16(conv_w_in, i),),
                             conv_w[i], None, _to_bf16(conv_w_out, i), seq=seq)
        h = _gated_block("ffn", h, ffn_norm_g[layer].reshape(1, -1),
                         (_to_bf16(ffn_w_gate, layer), _to_bf16(ffn_w_up, layer)),
                         ffn_dwconv_w[layer], ffn_dwconv_b[layer].reshape(1, -1),
                         _to_bf16(ffn_w_down, layer), seq=seq)
    return h.reshape(batch, seq, d)
```

```python
import functools
import math

import jax
import jax.numpy as jnp
from jax import lax
from jax.experimental import pallas as pl
from jax.experimental.pallas import tpu as pltpu

D_MODEL = 2048
A_HEADS = 8
A_HALF_DIM = 64
A_V_DIM = 128
B_HEADS = 8
B_NOPE_DIM = 128
B_ROPE_DIM = 64
B_QK_DIM = B_NOPE_DIM + B_ROPE_DIM
B_V_DIM = 128
B_Q_RANK = 512
B_KV_RANK = 256
B_QK_PAD = 256
ROPE_THETA = 10000.0
REL_BUCKETS = 32
REL_MAX_DIST = 128
EPS = 1e-6
LANES = 128
MXU_DIM = 256
V7X_VMEM_BYTES = 64 * 1024 * 1024
A_W = A_HEADS * 2 * A_HALF_DIM
LAT_W = B_Q_RANK + B_KV_RANK + LANES

HALO = 16
ATTN_TQ = 256
ATTN_HEAD_GROUP = 2
GATED_TM = 1024
GATED_SUB = {"ffn": 1024, "conv": 512}
LOG2E = math.log2(math.e)
VMEM_LIMIT = V7X_VMEM_BYTES - 2 * 1024 * 1024

_F32 = jnp.float32
_BF16 = jnp.bfloat16


def _rms(xf, g):
    ms = jnp.mean(xf * xf, axis=-1, keepdims=True)
    return xf * lax.rsqrt(ms + EPS) * g


def _dot(a, b):
    return jnp.dot(a, b, preferred_element_type=_F32)


def _dot_nt(a, b):
    return lax.dot_general(a, b, (((1,), (1,)), ((), ())), preferred_element_type=_F32)


def _params(sem):
    return pltpu.CompilerParams(dimension_semantics=sem, vmem_limit_bytes=VMEM_LIMIT)


def _full(a):
    return pl.BlockSpec(a.shape, lambda *_: (0,) * a.ndim)


def _cast_kernel(w_ref, o_ref):
    o_ref[...] = w_ref[...].astype(_BF16)


def _to_bf16(w, layer, *, tr=256):
    _, r, c = w.shape
    return pl.pallas_call(
        _cast_kernel,
        grid=(r // tr,),
        in_specs=[pl.BlockSpec((None, tr, c), lambda i: (layer, i, 0))],
        out_specs=pl.BlockSpec((tr, c), lambda i: (i, 0)),
        out_shape=jax.ShapeDtypeStruct((r, c), _BF16),
        compiler_params=_params(("parallel",)),
        name="weight_to_bf16",
    )(w)


def _norm_kernel(x_ref, g_ref, pos_ref, inv_ref, o_ref, rope_ref):
    o_ref[...] = _rms(x_ref[...], g_ref[...]).astype(_BF16)
    ang = pos_ref[...].astype(_F32) * inv_ref[...]
    lane = lax.broadcasted_iota(jnp.int32, (1, LANES), 1)
    half = B_ROPE_DIM // 2
    cosv = jnp.cos(ang)
    sinv = jnp.sin(ang)
    rope_ref[0] = jnp.where(lane < B_ROPE_DIM, cosv, 0.0)
    rope_ref[1] = jnp.where(lane < half, -sinv, 0.0)
    rope_ref[2] = jnp.where((lane >= half) & (lane < B_ROPE_DIM), sinv, 0.0)


def _norm(x, g, pos, inv, *, tm=1024):
    t = x.shape[0]
    return pl.pallas_call(
        _norm_kernel,
        grid=(t // tm,),
        in_specs=[pl.BlockSpec((tm, D_MODEL), lambda i: (i, 0)), _full(g),
                  pl.BlockSpec((tm, 1), lambda i: (i, 0)), _full(inv)],
        out_specs=[pl.BlockSpec((tm, D_MODEL), lambda i: (i, 0)),
                   pl.BlockSpec((3, tm, LANES), lambda i: (0, i, 0))],
        out_shape=[jax.ShapeDtypeStruct((t, D_MODEL), _BF16), jax.ShapeDtypeStruct((3, t, LANES), _F32)],
        compiler_params=_params(("parallel",)),
        name="pre_norm",
    )(x, g, pos, inv)


def _qkv_proj_kernel(h_ref, w32_ref, g_ref, bd_ref, o_ref, w_ref):
    j = pl.program_id(0)
    h = h_ref[...]

    @pl.when(pl.program_id(1) == 0)
    def _():
        w_ref[...] = w32_ref[...].astype(_BF16)

    @pl.when(j < 2)
    def _():
        bd = bd_ref[...]
        y = _dot_nt(h, w_ref[...])
        y2 = (y * y).astype(_BF16)
        width = bd.shape[0]
        for c in range(w_ref.shape[0] // width):
            cols = slice(c * width, (c + 1) * width)
            ms = _dot(y2[:, cols], bd) * (1.0 / A_HALF_DIM)
            o_ref[:, cols] = (y[:, cols] * lax.rsqrt(ms + EPS) * g_ref[0, :, cols]).astype(_BF16)

    @pl.when(j == 2)
    def _():
        o_ref[...] = _dot_nt(h, w_ref[...]).astype(_BF16)


def _qkv_proj(h, wt, layer, gains, *, tm=1024):
    t = h.shape[0]
    grp = jnp.arange(2 * MXU_DIM) // A_HALF_DIM
    bd = (grp[:, None] == grp[None, :]).astype(_BF16)
    return pl.pallas_call(
        _qkv_proj_kernel,
        grid=(3, t // tm),
        in_specs=[
            pl.BlockSpec((tm, D_MODEL), lambda j, i: (i, 0)),
            pl.BlockSpec((None, A_W, D_MODEL), lambda j, i: (layer, j, 0)),
            pl.BlockSpec((1, 1, A_W), lambda j, i: (jnp.minimum(j, 1), 0, 0)),
            _full(bd),
        ],
        out_specs=pl.BlockSpec((tm, A_W), lambda j, i: (i, j)),
        out_shape=jax.ShapeDtypeStruct((t, 3 * A_W), _BF16),
        scratch_shapes=[pltpu.VMEM((A_W, D_MODEL), _BF16)],
        compiler_params=_params(("arbitrary", "arbitrary")),
        name="diff_qkv_proj",
    )(h, wt, gains, bd)


def _mla_prep_kernel(h_ref, rope_ref, wl_ref, qag_ref, wuq_ref, kvag_ref, wukv_ref, mqg_ref, mkg_ref,
                     q_ref, k_ref, v_ref, *, sub):
    half = B_ROPE_DIM // 2
    scale = B_QK_DIM ** -0.5 * LOG2E
    mqg = mqg_ref[...]
    mkg = mkg_ref[...]
    kv_w = B_NOPE_DIM + B_V_DIM

    def project(lo):
        lat = _dot_nt(h_ref[lo:lo + sub], wl_ref[...])
        cq = lat[:, :B_Q_RANK]
        ckv = lat[:, B_Q_RANK:B_Q_RANK + B_KV_RANK]
        kr = lat[:, B_Q_RANK + B_KV_RANK:]
        qf = _dot(_rms(cq, qag_ref[...]).astype(_BF16), wuq_ref[...])
        kv = _dot(_rms(ckv, kvag_ref[...]).astype(_BF16), wukv_ref[...])
        return qf, kv, kr

    def finish(lo, qf, kv, kr):
        rows = slice(lo, lo + sub)
        c_tab, s_lo, s_hi = rope_ref[0, rows], rope_ref[1, rows], rope_ref[2, rows]

        def rope(r):
            return r * c_tab + pltpu.roll(r, LANES - half, 1) * s_lo + pltpu.roll(r, half, 1) * s_hi

        for h in range(B_HEADS):
            slab = qf[:, h * B_QK_PAD:(h + 1) * B_QK_PAD]
            ms = jnp.sum(slab * slab, axis=-1, keepdims=True) * (1.0 / B_QK_DIM)
            sn = slab * (lax.rsqrt(ms + EPS) * scale) * mqg
            q_ref[rows, h * B_QK_PAD:h * B_QK_PAD + B_NOPE_DIM] = sn[:, :B_NOPE_DIM].astype(_BF16)
            q_ref[rows, h * B_QK_PAD + B_NOPE_DIM:(h + 1) * B_QK_PAD] = rope(sn[:, B_NOPE_DIM:]).astype(_BF16)
        kr_ss = jnp.sum(kr * kr, axis=-1, keepdims=True)
        kr_rot = rope(kr * mkg[:, B_NOPE_DIM:])
        for h in range(B_HEADS):
            kn = kv[:, h * kv_w:h * kv_w + B_NOPE_DIM]
            ms = (jnp.sum(kn * kn, axis=-1, keepdims=True) + kr_ss) * (1.0 / B_QK_DIM)
            rs = lax.rsqrt(ms + EPS)
            k_ref[rows, h * B_QK_PAD:h * B_QK_PAD + B_NOPE_DIM] = (kn * rs * mkg[:, :B_NOPE_DIM]).astype(_BF16)
            k_ref[rows, h * B_QK_PAD + B_NOPE_DIM:(h + 1) * B_QK_PAD] = (kr_rot * rs).astype(_BF16)
            v_ref[rows, h * B_V_DIM:(h + 1) * B_V_DIM] = kv[:, h * kv_w + B_NOPE_DIM:(h + 1) * kv_w].astype(_BF16)

    starts = list(range(0, h_ref.shape[0], sub))
    nxt = project(starts[0])
    for n, lo in enumerate(starts):
        cur = nxt
        if n + 1 < len(starts):
            nxt = project(starts[n + 1])
        finish(lo, *cur)


def _mla_prep(h, rope_tabs, wl, qag, wuq, kvag, wukv, mqg, mkg, *, tm=1024, sub=256):
    t = h.shape[0]
    return pl.pallas_call(
        functools.partial(_mla_prep_kernel, sub=sub),
        grid=(t // tm,),
        in_specs=[
            pl.BlockSpec((tm, D_MODEL), lambda i: (i, 0)),
            pl.BlockSpec((3, tm, LANES), lambda i: (0, i, 0)),
            _full(wl), _full(qag), _full(wuq), _full(kvag), _full(wukv), _full(mqg), _full(mkg),
        ],
        out_specs=[
            pl.BlockSpec((tm, B_HEADS * B_QK_PAD), lambda i: (i, 0)),
            pl.BlockSpec((tm, B_HEADS * B_QK_PAD), lambda i: (i, 0)),
            pl.BlockSpec((tm, B_HEADS * B_V_DIM), lambda i: (i, 0)),
        ],
        out_shape=[
            jax.ShapeDtypeStruct((t, B_HEADS * B_QK_PAD), _BF16),
            jax.ShapeDtypeStruct((t, B_HEADS * B_QK_PAD), _BF16),
            jax.ShapeDtypeStruct((t, B_HEADS * B_V_DIM), _BF16),
        ],
        compiler_params=_params(("parallel",)),
        name="mla_prep",
    )(h, rope_tabs, wl, qag, wuq, kvag, wukv, mqg, mkg)


def _t5_bucket_of(rel):
    nb = REL_BUCKETS // 2
    max_exact = nb // 2
    n = jnp.abs(rel)
    large = jnp.full(rel.shape, max_exact, jnp.int32)
    for k in range(1, nb - max_exact):
        large = large + (n >= math.ceil(max_exact * 2.0 ** (k / 2.0))).astype(jnp.int32)
    return jnp.where(rel > 0, nb, 0) + jnp.where(n < max_exact, n, large)


def _bias_of_rel(tbl_ref, head, rel):
    bucket = _t5_bucket_of(rel)
    val = jnp.zeros(rel.shape, _F32)
    for c in range(REL_BUCKETS):
        val = jnp.where(bucket == c, tbl_ref[c * A_HEADS + head], val)
    return val * LOG2E


def _bias_band_kernel(tbl_ref, o_ref, *, seq, tq):
    h = pl.program_id(0)
    blk = REL_MAX_DIST
    row = lax.broadcasted_iota(jnp.int32, (blk, blk), 0)
    col = lax.broadcasted_iota(jnp.int32, (blk, blk), 1)
    tiles = {}
    for d in range(-2, 3):
        rel = d * blk + col - row if abs(d) < 2 else jnp.full((blk, blk), d * blk // 2, jnp.int32)
        tiles[d] = _bias_of_rel(tbl_ref, h, rel)
    shift = (seq - tq) // blk
    for rb in range(tq // blk):
        for cb in range((2 * seq - tq) // blk):
            d = max(-2, min(2, cb - rb - shift))
            o_ref[0, rb * blk:(rb + 1) * blk, cb * blk:(cb + 1) * blk] = tiles[d]


def _bias_band(rel_table, *, seq, tq):
    cols = 2 * seq - tq
    return pl.pallas_call(
        functools.partial(_bias_band_kernel, seq=seq, tq=tq),
        grid=(A_HEADS,),
        in_specs=[pl.BlockSpec(memory_space=pltpu.SMEM)],
        out_specs=pl.BlockSpec((1, tq, cols), lambda h: (h, 0, 0)),
        out_shape=jax.ShapeDtypeStruct((A_HEADS, tq, cols), _F32),
        compiler_params=_params(("arbitrary",)),
        name="rel_bias_band",
    )(rel_table.reshape(-1))


def _softmax_pv(s, va):
    m = jnp.max(s, axis=-1, keepdims=True)
    p = jnp.exp2(s - m).astype(_BF16)
    oa = _dot(p, va)
    dv = va.shape[1] // 2
    return oa[:, :dv] / oa[:, dv:]


def _stage_v(v_ref, va_ref, dv):
    for g in range(va_ref.shape[0]):
        va_ref[g, :, :dv] = v_ref[:, g * dv:(g + 1) * dv]
        va_ref[g, :, dv:] = jnp.ones((v_ref.shape[0], dv), _BF16)


def _diff_attn_kernel(tbl_ref, q_ref, k_ref, v_ref, bias_ref, kpos_ref, lq1_ref, lk1_ref, lq2_ref, lk2_ref,
                      sg_ref, o_ref, va_ref, *, tq, seq, lam_init, from_band):
    hg = va_ref.shape[0]
    _stage_v(v_ref, va_ref, A_V_DIM)
    lane = lax.broadcasted_iota(jnp.int32, (1, LANES), 1)
    lam = (jnp.exp(jnp.sum(lq1_ref[...] * lk1_ref[...], axis=-1, keepdims=True))
           - jnp.exp(jnp.sum(lq2_ref[...] * lk2_ref[...], axis=-1, keepdims=True)) + lam_init)
    sg = sg_ref[...] * (1.0 - lam_init)

    def qk(g, rows, half):
        cols = slice(g * LANES, (g + 1) * LANES)
        q = q_ref[rows, cols]
        keep = (lane >= A_HALF_DIM) if half else (lane < A_HALF_DIM)
        return _dot_nt(jnp.where(keep, q, jnp.zeros_like(q)), k_ref[:, cols])

    def store(g, rows, o1, o2):
        o_ref[rows, g * A_V_DIM:(g + 1) * A_V_DIM] = _rms(o1 - lam * o2, sg).astype(_BF16)

    if from_band:
        def logits(g, t, half):
            off = seq - (t + 1) * tq
            return qk(g, slice(t * tq, (t + 1) * tq), half) + bias_ref[g, :, off:off + seq]

        units = [(g, t, half) for g in range(hg) for t in range(seq // tq) for half in range(2)]
        s_next = logits(*units[0])
        for u, (g, t, half) in enumerate(units):
            s = s_next
            if u + 1 < len(units):
                s_next = logits(*units[u + 1])
            if half == 0:
                o1 = _softmax_pv(s, va_ref[g])
            else:
                store(g, slice(t * tq, (t + 1) * tq), o1, _softmax_pv(s, va_ref[g]))
    else:
        k_pos = kpos_ref[...]
        for g in range(hg):
            head = pl.program_id(0) * hg + g

            def tile(t, carry):
                rows = pl.ds(pl.multiple_of(t * tq, tq), tq)
                bias = _bias_of_rel(tbl_ref, head, k_pos - bias_ref[rows])
                store(g, rows, _softmax_pv(qk(g, rows, 0) + bias, va_ref[g]),
                      _softmax_pv(qk(g, rows, 1) + bias, va_ref[g]))
                return carry

            lax.fori_loop(0, seq // tq, tile, 0)


def _diff_attn(qkv, rel_table, positions, lq1, lk1, lq2, lk2, sg, *, seq, lam_init, tq, hg=ATTN_HEAD_GROUP):
    batch = positions.shape[0]
    groups = A_HEADS // hg
    w = hg * LANES

    def call(from_band, bias, bias_spec):
        return pl.pallas_call(
            functools.partial(_diff_attn_kernel, tq=tq, seq=seq, lam_init=lam_init, from_band=from_band),
            grid=(groups, batch),
            in_specs=[
                pl.BlockSpec(memory_space=pltpu.SMEM),
                pl.BlockSpec((seq, w), lambda h, b: (b, h)),
                pl.BlockSpec((seq, w), lambda h, b: (b, groups + h)),
                pl.BlockSpec((seq, w), lambda h, b: (b, 2 * groups + h)),
                bias_spec,
                pl.BlockSpec((None, 1, seq), lambda h, b: (b, 0, 0)),
                _full(lq1), _full(lk1), _full(lq2), _full(lk2), _full(sg),
            ],
            out_specs=pl.BlockSpec((seq, hg * A_V_DIM), lambda h, b: (b, h)),
            out_shape=jax.ShapeDtypeStruct((batch * seq, A_HEADS * A_V_DIM), _BF16),
            scratch_shapes=[pltpu.VMEM((hg, seq, 2 * A_V_DIM), _BF16)],
            compiler_params=_params(("parallel", "arbitrary")),
            name="diff_attn_band" if from_band else "diff_attn_general",
        )(rel_table.reshape(-1), qkv, qkv, qkv, bias, positions.reshape(batch, 1, seq), lq1, lk1, lq2, lk2, sg)

    def banded():
        band = _bias_band(rel_table, seq=seq, tq=tq)
        return call(True, band, pl.BlockSpec((hg, tq, band.shape[2]), lambda h, b: (h, 0, 0)))

    def general():
        return call(False, positions.reshape(batch * seq, 1), pl.BlockSpec((seq, 1), lambda h, b: (b, 0)))

    consecutive = jnp.all(positions[:, 1:] - positions[:, :-1] == 1)
    return lax.cond(consecutive, banded, general)


def _mla_attn_kernel(q_ref, k_ref, v_ref, o_ref, va_ref, *, tq, seq):
    _stage_v(v_ref, va_ref, B_V_DIM)

    def logits(g, t):
        cols = slice(g * B_QK_PAD, (g + 1) * B_QK_PAD)
        return _dot_nt(q_ref[t * tq:(t + 1) * tq, cols], k_ref[:, cols])

    units = [(g, t) for g in range(va_ref.shape[0]) for t in range(seq // tq)]
    s_next = logits(*units[0])
    for u, (g, t) in enumerate(units):
        s = s_next
        if u + 1 < len(units):
            s_next = logits(*units[u + 1])
        o_ref[t * tq:(t + 1) * tq, g * B_V_DIM:(g + 1) * B_V_DIM] = _softmax_pv(s, va_ref[g]).astype(_BF16)


def _mla_attn(q, k, v, *, batch, seq, tq, hg=ATTN_HEAD_GROUP):
    return pl.pallas_call(
        functools.partial(_mla_attn_kernel, tq=tq, seq=seq),
        grid=(B_HEADS // hg, batch),
        in_specs=[
            pl.BlockSpec((seq, hg * B_QK_PAD), lambda h, b: (b, h)),
            pl.BlockSpec((seq, hg * B_QK_PAD), lambda h, b: (b, h)),
            pl.BlockSpec((seq, hg * B_V_DIM), lambda h, b: (b, h)),
        ],
        out_specs=pl.BlockSpec((seq, hg * B_V_DIM), lambda h, b: (b, h)),
        out_shape=jax.ShapeDtypeStruct((batch * seq, B_HEADS * B_V_DIM), _BF16),
        scratch_shapes=[pltpu.VMEM((hg, seq, 2 * B_V_DIM), _BF16)],
        compiler_params=_params(("parallel", "arbitrary")),
        name="mla_attn",
    )(q, k, v)


def _out_proj_kernel(x_ref, oa_ref, ob_ref, wa_ref, wb_ref, o_ref):
    o_ref[...] = x_ref[...] + _dot(oa_ref[...], wa_ref[...]) + _dot(ob_ref[...], wb_ref[...])


def _out_proj(x, oa, ob, w, *, tm=512):
    t = x.shape[0]
    ka, kb = oa.shape[1], ob.shape[1]
    assert ka == kb and ka + kb == w.shape[0]
    return pl.pallas_call(
        _out_proj_kernel,
        grid=(t // tm,),
        in_specs=[
            pl.BlockSpec((tm, D_MODEL), lambda i: (i, 0)),
            pl.BlockSpec((tm, ka), lambda i: (i, 0)),
            pl.BlockSpec((tm, kb), lambda i: (i, 0)),
            pl.BlockSpec((ka, D_MODEL), lambda i: (0, 0)),
            pl.BlockSpec((kb, D_MODEL), lambda i: (1, 0)),
        ],
        out_specs=pl.BlockSpec((tm, D_MODEL), lambda i: (i, 0)),
        out_shape=jax.ShapeDtypeStruct((t, D_MODEL), _F32),
        compiler_params=_params(("parallel",)),
        name="attn_out_proj",
    )(x, oa, ob, w, w)


def _dwconv3(g_ext, cw, tm):
    n = g_ext.shape[0]
    prev = pltpu.roll(g_ext, 1, 0)[HALO:HALO + tm]
    nxt = pltpu.roll(g_ext, n - 1, 0)[HALO:HALO + tm]
    return cw[0:1] * prev + cw[1:2] * g_ext[HALO:HALO + tm] + cw[2:3] * nxt


def _gated_block_kernel(*refs, kind, tm, seq):
    if kind == "ffn":
        x_hbm, xp_ref, xn_ref, g_ref, wa_ref, wb_ref, cw_ref, cb_ref, wd_ref, o_ref, hn_ref, x_ref, sem = refs
    else:
        x_hbm, xp_ref, xn_ref, g_ref, wa_ref, wb_ref, wc_ref, cw_ref, wd_ref, o_ref, hn_ref, x_ref, sem = refs
    i = pl.program_id(0)
    j = pl.program_id(1)
    tiles_per_seq = seq // tm

    def x_copy(tile):
        return pltpu.make_async_copy(x_hbm.at[pl.ds(tile * tm, tm)], x_ref, sem)

    @pl.when((i == 0) & (j == 0))
    def _():
        x_copy(0).start()

    @pl.when(j == 0)
    def _():
        x_copy(i).wait()
        g = g_ref[...]
        pos_in_seq = i % tiles_per_seq
        keep_prev = (pos_in_seq != 0).astype(_F32)
        keep_next = (pos_in_seq != tiles_per_seq - 1).astype(_F32)
        hn_ref[0:HALO] = (_rms(xp_ref[...], g) * keep_prev).astype(_BF16)
        hn_ref[HALO:HALO + tm] = _rms(x_ref[...], g).astype(_BF16)
        hn_ref[HALO + tm:] = (_rms(xn_ref[...], g) * keep_next).astype(_BF16)
        o_ref[...] = x_ref[...]

    @pl.when((j == 1) & (i + 1 < pl.num_programs(0)))
    def _():
        x_copy(i + 1).start()

    cw = cw_ref[...]
    sub = min(tm, GATED_SUB[kind])
    for lo in range(0, tm, sub):
        h_ext = hn_ref[lo:lo + sub + 2 * HALO]
        h_main = hn_ref[lo + HALO:lo + HALO + sub]
        if kind == "ffn":
            c = _dwconv3(_dot(h_ext, wa_ref[...]), cw, sub) + cb_ref[...]
            act = c * jax.nn.sigmoid(c) * _dot(h_main, wb_ref[...])
        else:
            act = _dot(h_main, wa_ref[...]) * _dwconv3(_dot(h_ext, wb_ref[...]) * _dot(h_ext, wc_ref[...]), cw, sub)
        o_ref[lo:lo + sub] += _dot(act.astype(_BF16), wd_ref[...])


def _gated_block(kind, x, g, ups, cw, cb, wd, *, seq, tc=512):
    t = x.shape[0]
    tm = GATED_TM
    c_total = wd.shape[0]
    nc = c_total // tc
    hb = tm // HALO
    row_specs = [
        pl.BlockSpec(memory_space=pl.ANY),
        pl.BlockSpec((HALO, D_MODEL), lambda i, j: (jnp.maximum(i * hb - 1, 0), 0)),
        pl.BlockSpec((HALO, D_MODEL), lambda i, j: (jnp.minimum((i + 1) * hb, t // HALO - 1), 0)),
        pl.BlockSpec((1, D_MODEL), lambda i, j: (0, 0)),
    ]
    if kind == "ffn":
        wg, wu = ups
        args = (x, x, x, g, wg, wu, cw, cb, wd)
        w_specs = [
            pl.BlockSpec((D_MODEL, tc), lambda i, j: (0, j)),
            pl.BlockSpec((D_MODEL, tc), lambda i, j: (0, j)),
            pl.BlockSpec((3, tc), lambda i, j: (0, j)),
            pl.BlockSpec((1, tc), lambda i, j: (0, j)),
        ]
    else:
        (w_in,) = ups
        args = (x, x, x, g, w_in, w_in, w_in, cw, wd)
        w_specs = [
            pl.BlockSpec((D_MODEL, tc), lambda i, j: (0, j)),
            pl.BlockSpec((D_MODEL, tc), lambda i, j: (0, nc + j)),
            pl.BlockSpec((D_MODEL, tc), lambda i, j: (0, 2 * nc + j)),
            pl.BlockSpec((3, tc), lambda i, j: (0, j)),
        ]
    return pl.pallas_call(
        functools.partial(_gated_block_kernel, kind=kind, tm=tm, seq=seq),
        grid=(t // tm, nc),
        in_specs=row_specs + w_specs + [pl.BlockSpec((tc, D_MODEL), lambda i, j: (j, 0))],
        out_specs=pl.BlockSpec((tm, D_MODEL), lambda i, j: (i, 0)),
        out_shape=jax.ShapeDtypeStruct((t, D_MODEL), _F32),
        scratch_shapes=[pltpu.VMEM((tm + 2 * HALO, D_MODEL), _BF16), pltpu.VMEM((tm, D_MODEL), _F32),
                        pltpu.SemaphoreType.DMA(())],
        compiler_params=_params(("arbitrary", "arbitrary")),
        name=kind + "_block",
    )(*args)


def _pad_lanes(a, width):
    return jnp.pad(a, ((0, 0), (0, width - a.shape[1])))


def _attn_layer(x, positions, rel_table, layer_idx, norm_g, w_in_t, w_lat_t, dq_g, dk_g, lq1, lk1, lq2, lk2,
                subln_g, q_a_g, w_uq, kv_a_g, w_ukv, mq_g, mk_g, w_out, *, batch, seq):
    row = lambda a: a.reshape(1, -1)
    lam_init = 0.8 - 0.6 * math.exp(-0.3 * layer_idx)
    inv = 1.0 / (ROPE_THETA ** (jnp.arange(0, B_ROPE_DIM, 2, dtype=_F32) / B_ROPE_DIM))
    hn, rope_tabs = _norm(x, row(norm_g), positions.reshape(-1, 1), _pad_lanes(jnp.tile(inv, 2).reshape(1, -1), LANES))

    reps = A_W // A_HALF_DIM
    gains = jnp.stack([jnp.tile(dq_g, reps) * (A_HALF_DIM ** -0.5 * LOG2E), jnp.tile(dk_g, reps)])
    qkv = _qkv_proj(hn, w_in_t, layer_idx // 2, gains.reshape(2, 1, A_W))

    w_uq_p = jnp.pad(w_uq.reshape(B_Q_RANK, B_HEADS, B_QK_DIM),
                     ((0, 0), (0, 0), (0, B_QK_PAD - B_QK_DIM))).reshape(B_Q_RANK, -1).astype(_BF16)
    qm, km, vm = _mla_prep(hn, rope_tabs, w_lat_t, row(q_a_g), w_uq_p, row(kv_a_g), w_ukv.astype(_BF16),
                           _pad_lanes(row(mq_g), B_QK_PAD), _pad_lanes(row(mk_g), B_QK_PAD))

    oa = _diff_attn(qkv, rel_table, positions, row(lq1), row(lk1), row(lq2), row(lk2), row(subln_g),
                    seq=seq, lam_init=lam_init, tq=ATTN_TQ)
    ob = _mla_attn(qm, km, vm, batch=batch, seq=seq, tq=ATTN_TQ)
    return _out_proj(x, oa, ob, w_out)


def kernel(x, positions, rel_bias_table, attn_norm_g, attn_w_in, diff_q_norm_g, diff_k_norm_g, diff_lambda_q1, diff_lambda_k1, diff_lambda_q2, diff_lambda_k2, diff_subln_g, mla_q_a_norm_g, mla_w_uq, mla_kv_a_norm_g, mla_w_ukv, mla_q_norm_g, mla_k_norm_g, attn_w_out, conv_norm_g, conv_w_in, conv_w, conv_w_out, ffn_norm_g, ffn_w_gate, ffn_w_up, ffn_dwconv_w, ffn_dwconv_b, ffn_w_down):
    batch, seq, d = x.shape
    depth = ffn_norm_g.shape[0]
    h = x.reshape(batch * seq, d)
    for layer in range(depth):
        i = layer // 2
        if layer % 2 == 0:
            w_in_t = jnp.swapaxes(attn_w_in, 1, 2)
            lat_rows = w_in_t.shape[1] - 3 * A_W
            w_lat_t = jnp.pad(w_in_t[i, 3 * A_W:], ((0, LAT_W - lat_rows), (0, 0))).astype(_BF16)
            h = _attn_layer(h, positions, rel_bias_table, layer, attn_norm_g[i], w_in_t, w_lat_t, diff_q_norm_g[i],
                            diff_k_norm_g[i], diff_lambda_q1[i], diff_lambda_k1[i], diff_lambda_q2[i],
                            diff_lambda_k2[i], diff_subln_g[i], mla_q_a_norm_g[i], mla_w_uq[i],
                            mla_kv_a_norm_g[i], mla_w_ukv[i], mla_q_norm_g[i], mla_k_norm_g[i],
                            _to_bf16(attn_w_out, i), batch=batch, seq=seq)
        else:
            h = _gated_block("conv", h, conv_norm_g[i].reshape(1, -1), (_to_bf16(conv_w_in, i),),
                             conv_w[i], None, _to_bf16(conv_w_out, i), seq=seq)
        h = _gated_block("ffn", h, ffn_norm_g[layer].reshape(1, -1),
                         (_to_bf16(ffn_w_gate, layer), _to_bf16(ffn_w_up, layer)),
                         ffn_dwconv_w[layer], ffn_dwconv_b[layer].reshape(1, -1),
                         _to_bf16(ffn_w_down, layer), seq=seq)
    return h.reshape(batch, seq, d)
```

```python
import functools
import math

import jax
import jax.numpy as jnp
from jax import lax
from jax.experimental import pallas as pl
from jax.experimental.pallas import tpu as pltpu

D_MODEL = 2048
A_HEADS = 8
A_HALF_DIM = 64
A_V_DIM = 128
B_HEADS = 8
B_NOPE_DIM = 128
B_ROPE_DIM = 64
B_QK_DIM = B_NOPE_DIM + B_ROPE_DIM
B_V_DIM = 128
B_Q_RANK = 512
B_KV_RANK = 256
B_QK_PAD = 256
ROPE_THETA = 10000.0
REL_BUCKETS = 32
REL_MAX_DIST = 128
EPS = 1e-6
LANES = 128
MXU_DIM = 256
V7X_VMEM_BYTES = 64 * 1024 * 1024
A_W = A_HEADS * 2 * A_HALF_DIM
LAT_W = B_Q_RANK + B_KV_RANK + LANES

HALO = 16
ATTN_TQ = 256
ATTN_HEAD_GROUP = 2
GATED_TM = 1024
GATED_SUB = 512
LOG2E = math.log2(math.e)
VMEM_LIMIT = V7X_VMEM_BYTES - 2 * 1024 * 1024

_F32 = jnp.float32
_BF16 = jnp.bfloat16


def _rms(xf, g):
    ms = jnp.mean(xf * xf, axis=-1, keepdims=True)
    return xf * lax.rsqrt(ms + EPS) * g


def _dot(a, b):
    return jnp.dot(a, b, preferred_element_type=_F32)


def _dot_nt(a, b):
    return lax.dot_general(a, b, (((1,), (1,)), ((), ())), preferred_element_type=_F32)


def _params(sem):
    return pltpu.CompilerParams(dimension_semantics=sem, vmem_limit_bytes=VMEM_LIMIT)


def _full(a):
    return pl.BlockSpec(a.shape, lambda *_: (0,) * a.ndim)


def _cast_kernel(w_ref, o_ref):
    o_ref[...] = w_ref[...].astype(_BF16)


def _to_bf16(w, layer, *, tr=256):
    _, r, c = w.shape
    return pl.pallas_call(
        _cast_kernel,
        grid=(r // tr,),
        in_specs=[pl.BlockSpec((None, tr, c), lambda i: (layer, i, 0))],
        out_specs=pl.BlockSpec((tr, c), lambda i: (i, 0)),
        out_shape=jax.ShapeDtypeStruct((r, c), _BF16),
        compiler_params=_params(("parallel",)),
        name="weight_to_bf16",
    )(w)


def _norm_kernel(x_ref, g_ref, pos_ref, inv_ref, o_ref, rope_ref):
    o_ref[...] = _rms(x_ref[...], g_ref[...]).astype(_BF16)
    ang = pos_ref[...].astype(_F32) * inv_ref[...]
    lane = lax.broadcasted_iota(jnp.int32, (1, LANES), 1)
    half = B_ROPE_DIM // 2
    cosv = jnp.cos(ang)
    sinv = jnp.sin(ang)
    rope_ref[0] = jnp.where(lane < B_ROPE_DIM, cosv, 0.0)
    rope_ref[1] = jnp.where(lane < half, -sinv, 0.0)
    rope_ref[2] = jnp.where((lane >= half) & (lane < B_ROPE_DIM), sinv, 0.0)


def _norm(x, g, pos, inv, *, tm=1024):
    t = x.shape[0]
    return pl.pallas_call(
        _norm_kernel,
        grid=(t // tm,),
        in_specs=[pl.BlockSpec((tm, D_MODEL), lambda i: (i, 0)), _full(g),
                  pl.BlockSpec((tm, 1), lambda i: (i, 0)), _full(inv)],
        out_specs=[pl.BlockSpec((tm, D_MODEL), lambda i: (i, 0)),
                   pl.BlockSpec((3, tm, LANES), lambda i: (0, i, 0))],
        out_shape=[jax.ShapeDtypeStruct((t, D_MODEL), _BF16), jax.ShapeDtypeStruct((3, t, LANES), _F32)],
        compiler_params=_params(("parallel",)),
        name="pre_norm",
    )(x, g, pos, inv)


def _qkv_proj_kernel(h_ref, w32_ref, g_ref, bd_ref, o_ref, w_ref):
    j = pl.program_id(0)
    h = h_ref[...]

    @pl.when(pl.program_id(1) == 0)
    def _():
        w_ref[...] = w32_ref[...].astype(_BF16)

    @pl.when(j < 2)
    def _():
        bd = bd_ref[...]
        y = _dot_nt(h, w_ref[...])
        y2 = (y * y).astype(_BF16)
        width = bd.shape[0]
        for c in range(w_ref.shape[0] // width):
            cols = slice(c * width, (c + 1) * width)
            ms = _dot(y2[:, cols], bd) * (1.0 / A_HALF_DIM)
            o_ref[:, cols] = (y[:, cols] * lax.rsqrt(ms + EPS) * g_ref[0, :, cols]).astype(_BF16)

    @pl.when(j == 2)
    def _():
        o_ref[...] = _dot_nt(h, w_ref[...]).astype(_BF16)


def _qkv_proj(h, wt, layer, gains, *, tm=1024):
    t = h.shape[0]
    grp = jnp.arange(2 * MXU_DIM) // A_HALF_DIM
    bd = (grp[:, None] == grp[None, :]).astype(_BF16)
    return pl.pallas_call(
        _qkv_proj_kernel,
        grid=(3, t // tm),
        in_specs=[
            pl.BlockSpec((tm, D_MODEL), lambda j, i: (i, 0)),
            pl.BlockSpec((None, A_W, D_MODEL), lambda j, i: (layer, j, 0)),
            pl.BlockSpec((1, 1, A_W), lambda j, i: (jnp.minimum(j, 1), 0, 0)),
            _full(bd),
        ],
        out_specs=pl.BlockSpec((tm, A_W), lambda j, i: (i, j)),
        out_shape=jax.ShapeDtypeStruct((t, 3 * A_W), _BF16),
        scratch_shapes=[pltpu.VMEM((A_W, D_MODEL), _BF16)],
        compiler_params=_params(("arbitrary", "arbitrary")),
        name="diff_qkv_proj",
    )(h, wt, gains, bd)


def _mla_prep_kernel(h_ref, rope_ref, wl_ref, qag_ref, wuq_ref, kvag_ref, wukv_ref, mqg_ref, mkg_ref,
                     q_ref, k_ref, v_ref, *, sub):
    half = B_ROPE_DIM // 2
    scale = B_QK_DIM ** -0.5 * LOG2E
    mqg = mqg_ref[...]
    mkg = mkg_ref[...]
    kv_w = B_NOPE_DIM + B_V_DIM

    def project(lo):
        lat = _dot_nt(h_ref[lo:lo + sub], wl_ref[...])
        cq = lat[:, :B_Q_RANK]
        ckv = lat[:, B_Q_RANK:B_Q_RANK + B_KV_RANK]
        kr = lat[:, B_Q_RANK + B_KV_RANK:]
        qf = _dot(_rms(cq, qag_ref[...]).astype(_BF16), wuq_ref[...])
        kv = _dot(_rms(ckv, kvag_ref[...]).astype(_BF16), wukv_ref[...])
        return qf, kv, kr

    def finish(lo, qf, kv, kr):
        rows = slice(lo, lo + sub)
        c_tab, s_lo, s_hi = rope_ref[0, rows], rope_ref[1, rows], rope_ref[2, rows]

        def rope(r):
            return r * c_tab + pltpu.roll(r, LANES - half, 1) * s_lo + pltpu.roll(r, half, 1) * s_hi

        for h in range(B_HEADS):
            slab = qf[:, h * B_QK_PAD:(h + 1) * B_QK_PAD]
            ms = jnp.sum(slab * slab, axis=-1, keepdims=True) * (1.0 / B_QK_DIM)
            sn = slab * (lax.rsqrt(ms + EPS) * scale) * mqg
            q_ref[rows, h * B_QK_PAD:h * B_QK_PAD + B_NOPE_DIM] = sn[:, :B_NOPE_DIM].astype(_BF16)
            q_ref[rows, h * B_QK_PAD + B_NOPE_DIM:(h + 1) * B_QK_PAD] = rope(sn[:, B_NOPE_DIM:]).astype(_BF16)
        kr_ss = jnp.sum(kr * kr, axis=-1, keepdims=True)
        kr_rot = rope(kr * mkg[:, B_NOPE_DIM:])
        for h in range(B_HEADS):
            kn = kv[:, h * kv_w:h * kv_w + B_NOPE_DIM]
            ms = (jnp.sum(kn * kn, axis=-1, keepdims=True) + kr_ss) * (1.0 / B_QK_DIM)
            rs = lax.rsqrt(ms + EPS)
            k_ref[rows, h * B_QK_PAD:h * B_QK_PAD + B_NOPE_DIM] = (kn * rs * mkg[:, :B_NOPE_DIM]).astype(_BF16)
            k_ref[rows, h * B_QK_PAD + B_NOPE_DIM:(h + 1) * B_QK_PAD] = (kr_rot * rs).astype(_BF16)
            v_ref[rows, h * B_V_DIM:(h + 1) * B_V_DIM] = kv[:, h * kv_w + B_NOPE_DIM:(h + 1) * kv_w].astype(_BF16)

    starts = list(range(0, h_ref.shape[0], sub))
    nxt = project(starts[0])
    for n, lo in enumerate(starts):
        cur = nxt
        if n + 1 < len(starts):
            nxt = project(starts[n + 1])
        finish(lo, *cur)


def _mla_prep(h, rope_tabs, wl, qag, wuq, kvag, wukv, mqg, mkg, *, tm=1024, sub=256):
    t = h.shape[0]
    return pl.pallas_call(
        functools.partial(_mla_prep_kernel, sub=sub),
        grid=(t // tm,),
        in_specs=[
            pl.BlockSpec((tm, D_MODEL), lambda i: (i, 0)),
            pl.BlockSpec((3, tm, LANES), lambda i: (0, i, 0)),
            _full(wl), _full(qag), _full(wuq), _full(kvag), _full(wukv), _full(mqg), _full(mkg),
        ],
        out_specs=[
            pl.BlockSpec((tm, B_HEADS * B_QK_PAD), lambda i: (i, 0)),
            pl.BlockSpec((tm, B_HEADS * B_QK_PAD), lambda i: (i, 0)),
            pl.BlockSpec((tm, B_HEADS * B_V_DIM), lambda i: (i, 0)),
        ],
        out_shape=[
            jax.ShapeDtypeStruct((t, B_HEADS * B_QK_PAD), _BF16),
            jax.ShapeDtypeStruct((t, B_HEADS * B_QK_PAD), _BF16),
            jax.ShapeDtypeStruct((t, B_HEADS * B_V_DIM), _BF16),
        ],
        compiler_params=_params(("parallel",)),
        name="mla_prep",
    )(h, rope_tabs, wl, qag, wuq, kvag, wukv, mqg, mkg)


def _t5_bucket_of(rel):
    nb = REL_BUCKETS // 2
    max_exact = nb // 2
    n = jnp.abs(rel)
    large = jnp.full(rel.shape, max_exact, jnp.int32)
    for k in range(1, nb - max_exact):
        large = large + (n >= math.ceil(max_exact * 2.0 ** (k / 2.0))).astype(jnp.int32)
    return jnp.where(rel > 0, nb, 0) + jnp.where(n < max_exact, n, large)


def _bias_of_rel(tbl_ref, head, rel):
    bucket = _t5_bucket_of(rel)
    val = jnp.zeros(rel.shape, _F32)
    for c in range(REL_BUCKETS):
        val = jnp.where(bucket == c, tbl_ref[c * A_HEADS + head], val)
    return val * LOG2E


def _bias_band_kernel(tbl_ref, o_ref, *, seq, tq):
    h = pl.program_id(0)
    blk = REL_MAX_DIST
    row = lax.broadcasted_iota(jnp.int32, (blk, blk), 0)
    col = lax.broadcasted_iota(jnp.int32, (blk, blk), 1)
    tiles = {}
    for d in range(-2, 3):
        rel = d * blk + col - row if abs(d) < 2 else jnp.full((blk, blk), d * blk // 2, jnp.int32)
        tiles[d] = _bias_of_rel(tbl_ref, h, rel)
    shift = (seq - tq) // blk
    for rb in range(tq // blk):
        for cb in range((2 * seq - tq) // blk):
            d = max(-2, min(2, cb - rb - shift))
            o_ref[0, rb * blk:(rb + 1) * blk, cb * blk:(cb + 1) * blk] = tiles[d]


def _bias_band(rel_table, *, seq, tq):
    cols = 2 * seq - tq
    return pl.pallas_call(
        functools.partial(_bias_band_kernel, seq=seq, tq=tq),
        grid=(A_HEADS,),
        in_specs=[pl.BlockSpec(memory_space=pltpu.SMEM)],
        out_specs=pl.BlockSpec((1, tq, cols), lambda h: (h, 0, 0)),
        out_shape=jax.ShapeDtypeStruct((A_HEADS, tq, cols), _F32),
        compiler_params=_params(("arbitrary",)),
        name="rel_bias_band",
    )(rel_table.reshape(-1))


def _softmax_pv(s, va):
    m = jnp.max(s, axis=-1, keepdims=True)
    p = jnp.exp2(s - m).astype(_BF16)
    oa = _dot(p, va)
    dv = va.shape[1] // 2
    return oa[:, :dv] / oa[:, dv:]


def _stage_v(v_ref, va_ref, dv):
    for g in range(va_ref.shape[0]):
        va_ref[g, :, :dv] = v_ref[:, g * dv:(g + 1) * dv]
        va_ref[g, :, dv:] = jnp.ones((v_ref.shape[0], dv), _BF16)


def _diff_attn_kernel(tbl_ref, q_ref, k_ref, v_ref, bias_ref, kpos_ref, lq1_ref, lk1_ref, lq2_ref, lk2_ref,
                      sg_ref, o_ref, va_ref, *, tq, seq, lam_init, from_band):
    hg = va_ref.shape[0]
    _stage_v(v_ref, va_ref, A_V_DIM)
    lane = lax.broadcasted_iota(jnp.int32, (1, LANES), 1)
    lam = (jnp.exp(jnp.sum(lq1_ref[...] * lk1_ref[...], axis=-1, keepdims=True))
           - jnp.exp(jnp.sum(lq2_ref[...] * lk2_ref[...], axis=-1, keepdims=True)) + lam_init)
    sg = sg_ref[...] * (1.0 - lam_init)

    def qk(g, rows, half):
        cols = slice(g * LANES, (g + 1) * LANES)
        q = q_ref[rows, cols]
        keep = (lane >= A_HALF_DIM) if half else (lane < A_HALF_DIM)
        return _dot_nt(jnp.where(keep, q, jnp.zeros_like(q)), k_ref[:, cols])

    def store(g, rows, o1, o2):
        o_ref[rows, g * A_V_DIM:(g + 1) * A_V_DIM] = _rms(o1 - lam * o2, sg).astype(_BF16)

    if from_band:
        def logits(g, t, half):
            off = seq - (t + 1) * tq
            return qk(g, slice(t * tq, (t + 1) * tq), half) + bias_ref[g, :, off:off + seq]

        units = [(g, t, half) for g in range(hg) for t in range(seq // tq) for half in range(2)]
        s_next = logits(*units[0])
        for u, (g, t, half) in enumerate(units):
            s = s_next
            if u + 1 < len(units):
                s_next = logits(*units[u + 1])
            if half == 0:
                o1 = _softmax_pv(s, va_ref[g])
            else:
                store(g, slice(t * tq, (t + 1) * tq), o1, _softmax_pv(s, va_ref[g]))
    else:
        k_pos = kpos_ref[...]
        for g in range(hg):
            head = pl.program_id(0) * hg + g

            def tile(t, carry):
                rows = pl.ds(pl.multiple_of(t * tq, tq), tq)
                bias = _bias_of_rel(tbl_ref, head, k_pos - bias_ref[rows])
                store(g, rows, _softmax_pv(qk(g, rows, 0) + bias, va_ref[g]),
                      _softmax_pv(qk(g, rows, 1) + bias, va_ref[g]))
                return carry

            lax.fori_loop(0, seq // tq, tile, 0)


def _diff_attn(qkv, rel_table, positions, lq1, lk1, lq2, lk2, sg, *, seq, lam_init, tq, hg=ATTN_HEAD_GROUP):
    batch = positions.shape[0]
    groups = A_HEADS // hg
    w = hg * LANES

    def call(from_band, bias, bias_spec):
        return pl.pallas_call(
            functools.partial(_diff_attn_kernel, tq=tq, seq=seq, lam_init=lam_init, from_band=from_band),
            grid=(groups, batch),
            in_specs=[
                pl.BlockSpec(memory_space=pltpu.SMEM),
                pl.BlockSpec((seq, w), lambda h, b: (b, h)),
                pl.BlockSpec((seq, w), lambda h, b: (b, groups + h)),
                pl.BlockSpec((seq, w), lambda h, b: (b, 2 * groups + h)),
                bias_spec,
                pl.BlockSpec((None, 1, seq), lambda h, b: (b, 0, 0)),
                _full(lq1), _full(lk1), _full(lq2), _full(lk2), _full(sg),
            ],
            out_specs=pl.BlockSpec((seq, hg * A_V_DIM), lambda h, b: (b, h)),
            out_shape=jax.ShapeDtypeStruct((batch * seq, A_HEADS * A_V_DIM), _BF16),
            scratch_shapes=[pltpu.VMEM((hg, seq, 2 * A_V_DIM), _BF16)],
            compiler_params=_params(("parallel", "arbitrary")),
            name="diff_attn_band" if from_band else "diff_attn_general",
        )(rel_table.reshape(-1), qkv, qkv, qkv, bias, positions.reshape(batch, 1, seq), lq1, lk1, lq2, lk2, sg)

    def banded():
        band = _bias_band(rel_table, seq=seq, tq=tq)
        return call(True, band, pl.BlockSpec((hg, tq, band.shape[2]), lambda h, b: (h, 0, 0)))

    def general():
        return call(False, positions.reshape(batch * seq, 1), pl.BlockSpec((seq, 1), lambda h, b: (b, 0)))

    consecutive = jnp.all(positions[:, 1:] - positions[:, :-1] == 1)
    return lax.cond(consecutive, banded, general)


def _mla_attn_kernel(q_ref, k_ref, v_ref, o_ref, va_ref, *, tq, seq):
    _stage_v(v_ref, va_ref, B_V_DIM)

    def logits(g, t):
        cols = slice(g * B_QK_PAD, (g + 1) * B_QK_PAD)
        return _dot_nt(q_ref[t * tq:(t + 1) * tq, cols], k_ref[:, cols])

    units = [(g, t) for g in range(va_ref.shape[0]) for t in range(seq // tq)]
    s_next = logits(*units[0])
    for u, (g, t) in enumerate(units):
        s = s_next
        if u + 1 < len(units):
            s_next = logits(*units[u + 1])
        o_ref[t * tq:(t + 1) * tq, g * B_V_DIM:(g + 1) * B_V_DIM] = _softmax_pv(s, va_ref[g]).astype(_BF16)


def _mla_attn(q, k, v, *, batch, seq, tq, hg=ATTN_HEAD_GROUP):
    return pl.pallas_call(
        functools.partial(_mla_attn_kernel, tq=tq, seq=seq),
        grid=(B_HEADS // hg, batch),
        in_specs=[
            pl.BlockSpec((seq, hg * B_QK_PAD), lambda h, b: (b, h)),
            pl.BlockSpec((seq, hg * B_QK_PAD), lambda h, b: (b, h)),
            pl.BlockSpec((seq, hg * B_V_DIM), lambda h, b: (b, h)),
        ],
        out_specs=pl.BlockSpec((seq, hg * B_V_DIM), lambda h, b: (b, h)),
        out_shape=jax.ShapeDtypeStruct((batch * seq, B_HEADS * B_V_DIM), _BF16),
        scratch_shapes=[pltpu.VMEM((hg, seq, 2 * B_V_DIM), _BF16)],
        compiler_params=_params(("parallel", "arbitrary")),
        name="mla_attn",
    )(q, k, v)


def _out_proj_kernel(x_ref, oa_ref, ob_ref, wa_ref, wb_ref, o_ref):
    o_ref[...] = x_ref[...] + _dot(oa_ref[...], wa_ref[...]) + _dot(ob_ref[...], wb_ref[...])


def _out_proj(x, oa, ob, w, *, tm=512):
    t = x.shape[0]
    ka, kb = oa.shape[1], ob.shape[1]
    assert ka == kb and ka + kb == w.shape[0]
    return pl.pallas_call(
        _out_proj_kernel,
        grid=(t // tm,),
        in_specs=[
            pl.BlockSpec((tm, D_MODEL), lambda i: (i, 0)),
            pl.BlockSpec((tm, ka), lambda i: (i, 0)),
            pl.BlockSpec((tm, kb), lambda i: (i, 0)),
            pl.BlockSpec((ka, D_MODEL), lambda i: (0, 0)),
            pl.BlockSpec((kb, D_MODEL), lambda i: (1, 0)),
        ],
        out_specs=pl.BlockSpec((tm, D_MODEL), lambda i: (i, 0)),
        out_shape=jax.ShapeDtypeStruct((t, D_MODEL), _F32),
        compiler_params=_params(("parallel",)),
        name="attn_out_proj",
    )(x, oa, ob, w, w)


def _dwconv3(g_ext, cw, tm):
    n = g_ext.shape[0]
    prev = pltpu.roll(g_ext, 1, 0)[HALO:HALO + tm]
    nxt = pltpu.roll(g_ext, n - 1, 0)[HALO:HALO + tm]
    return cw[0:1] * prev + cw[1:2] * g_ext[HALO:HALO + tm] + cw[2:3] * nxt


def _gated_block_kernel(*refs, kind, tm, seq):
    if kind == "ffn":
        x_hbm, xp_ref, xn_ref, g_ref, wa_ref, wb_ref, cw_ref, cb_ref, wd_ref, o_ref, hn_ref, x_ref, sem = refs
    else:
        x_hbm, xp_ref, xn_ref, g_ref, wa_ref, wb_ref, wc_ref, cw_ref, wd_ref, o_ref, hn_ref, x_ref, sem = refs
    i = pl.program_id(0)
    j = pl.program_id(1)
    tiles_per_seq = seq // tm

    def x_copy(tile):
        return pltpu.make_async_copy(x_hbm.at[pl.ds(tile * tm, tm)], x_ref, sem)

    @pl.when((i == 0) & (j == 0))
    def _():
        x_copy(0).start()

    @pl.when(j == 0)
    def _():
        x_copy(i).wait()
        g = g_ref[...]
        pos_in_seq = i % tiles_per_seq
        keep_prev = (pos_in_seq != 0).astype(_F32)
        keep_next = (pos_in_seq != tiles_per_seq - 1).astype(_F32)
        hn_ref[0:HALO] = (_rms(xp_ref[...], g) * keep_prev).astype(_BF16)
        hn_ref[HALO:HALO + tm] = _rms(x_ref[...], g).astype(_BF16)
        hn_ref[HALO + tm:] = (_rms(xn_ref[...], g) * keep_next).astype(_BF16)
        o_ref[...] = x_ref[...]

    @pl.when((j == 1) & (i + 1 < pl.num_programs(0)))
    def _():
        x_copy(i + 1).start()

    cw = cw_ref[...]
    sub = min(tm, GATED_SUB)
    for lo in range(0, tm, sub):
        h_ext = hn_ref[lo:lo + sub + 2 * HALO]
        h_main = hn_ref[lo + HALO:lo + HALO + sub]
        if kind == "ffn":
            c = _dwconv3(_dot(h_ext, wa_ref[...]), cw, sub) + cb_ref[...]
            act = c * jax.nn.sigmoid(c) * _dot(h_main, wb_ref[...])
        else:
            act = _dot(h_main, wa_ref[...]) * _dwconv3(_dot(h_ext, wb_ref[...]) * _dot(h_ext, wc_ref[...]), cw, sub)
        o_ref[lo:lo + sub] += _dot(act.astype(_BF16), wd_ref[...])


def _gated_block(kind, x, g, ups, cw, cb, wd, *, seq, tc=512):
    t = x.shape[0]
    tm = GATED_TM
    c_total = wd.shape[0]
    nc = c_total // tc
    hb = tm // HALO
    row_specs = [
        pl.BlockSpec(memory_space=pl.ANY),
        pl.BlockSpec((HALO, D_MODEL), lambda i, j: (jnp.maximum(i * hb - 1, 0), 0)),
        pl.BlockSpec((HALO, D_MODEL), lambda i, j: (jnp.minimum((i + 1) * hb, t // HALO - 1), 0)),
        pl.BlockSpec((1, D_MODEL), lambda i, j: (0, 0)),
    ]
    if kind == "ffn":
        wg, wu = ups
        args = (x, x, x, g, wg, wu, cw, cb, wd)
        w_specs = [
            pl.BlockSpec((D_MODEL, tc), lambda i, j: (0, j)),
            pl.BlockSpec((D_MODEL, tc), lambda i, j: (0, j)),
            pl.BlockSpec((3, tc), lambda i, j: (0, j)),
            pl.BlockSpec((1, tc), lambda i, j: (0, j)),
        ]
    else:
        (w_in,) = ups
        args = (x, x, x, g, w_in, w_in, w_in, cw, wd)
        w_specs = [
            pl.BlockSpec((D_MODEL, tc), lambda i, j: (0, j)),
            pl.BlockSpec((D_MODEL, tc), lambda i, j: (0, nc + j)),
            pl.BlockSpec((D_MODEL, tc), lambda i, j: (0, 2 * nc + j)),
            pl.BlockSpec((3, tc), lambda i, j: (0, j)),
        ]
    return pl.pallas_call(
        functools.partial(_gated_block_kernel, kind=kind, tm=tm, seq=seq),
        grid=(t // tm, nc),
        in_specs=row_specs + w_specs + [pl.BlockSpec((tc, D_MODEL), lambda i, j: (j, 0))],
        out_specs=pl.BlockSpec((tm, D_MODEL), lambda i, j: (i, 0)),
        out_shape=jax.ShapeDtypeStruct((t, D_MODEL), _F32),
        scratch_shapes=[pltpu.VMEM((tm + 2 * HALO, D_MODEL), _BF16), pltpu.VMEM((tm, D_MODEL), _F32),
                        pltpu.SemaphoreType.DMA(())],
        compiler_params=_params(("arbitrary", "arbitrary")),
        name=kind + "_block",
    )(*args)


def _pad_lanes(a, width):
    return jnp.pad(a, ((0, 0), (0, width - a.shape[1])))


def _attn_layer(x, positions, rel_table, layer_idx, norm_g, w_in_t, w_lat_t, dq_g, dk_g, lq1, lk1, lq2, lk2,
                subln_g, q_a_g, w_uq, kv_a_g, w_ukv, mq_g, mk_g, w_out, *, batch, seq):
    row = lambda a: a.reshape(1, -1)
    lam_init = 0.8 - 0.6 * math.exp(-0.3 * layer_idx)
    inv = 1.0 / (ROPE_THETA ** (jnp.arange(0, B_ROPE_DIM, 2, dtype=_F32) / B_ROPE_DIM))
    hn, rope_tabs = _norm(x, row(norm_g), positions.reshape(-1, 1), _pad_lanes(jnp.tile(inv, 2).reshape(1, -1), LANES))

    reps = A_W // A_HALF_DIM
    gains = jnp.stack([jnp.tile(dq_g, reps) * (A_HALF_DIM ** -0.5 * LOG2E), jnp.tile(dk_g, reps)])
    qkv = _qkv_proj(hn, w_in_t, layer_idx // 2, gains.reshape(2, 1, A_W))

    w_uq_p = jnp.pad(w_uq.reshape(B_Q_RANK, B_HEADS, B_QK_DIM),
                     ((0, 0), (0, 0), (0, B_QK_PAD - B_QK_DIM))).reshape(B_Q_RANK, -1).astype(_BF16)
    qm, km, vm = _mla_prep(hn, rope_tabs, w_lat_t, row(q_a_g), w_uq_p, row(kv_a_g), w_ukv.astype(_BF16),
                           _pad_lanes(row(mq_g), B_QK_PAD), _pad_lanes(row(mk_g), B_QK_PAD))

    oa = _diff_attn(qkv, rel_table, positions, row(lq1), row(lk1), row(lq2), row(lk2), row(subln_g),
                    seq=seq, lam_init=lam_init, tq=ATTN_TQ)
    ob = _mla_attn(qm, km, vm, batch=batch, seq=seq, tq=ATTN_TQ)
    return _out_proj(x, oa, ob, w_out)


def kernel(x, positions, rel_bias_table, attn_norm_g, attn_w_in, diff_q_norm_g, diff_k_norm_g, diff_lambda_q1, diff_lambda_k1, diff_lambda_q2, diff_lambda_k2, diff_subln_g, mla_q_a_norm_g, mla_w_uq, mla_kv_a_norm_g, mla_w_ukv, mla_q_norm_g, mla_k_norm_g, attn_w_out, conv_norm_g, conv_w_in, conv_w, conv_w_out, ffn_norm_g, ffn_w_gate, ffn_w_up, ffn_dwconv_w, ffn_dwconv_b, ffn_w_down):
    batch, seq, d = x.shape
    depth = ffn_norm_g.shape[0]
    h = x.reshape(batch * seq, d)
    for layer in range(depth):
        i = layer // 2
        if layer % 2 == 0:
            w_in_t = jnp.swapaxes(attn_w_in, 1, 2)
            lat_rows = w_in_t.shape[1] - 3 * A_W
            w_lat_t = jnp.pad(w_in_t[i, 3 * A_W:], ((0, LAT_W - lat_rows), (0, 0))).astype(_BF16)
            h = _attn_layer(h, positions, rel_bias_table, layer, attn_norm_g[i], w_in_t, w_lat_t, diff_q_norm_g[i],
                            diff_k_norm_g[i], diff_lambda_q1[i], diff_lambda_k1[i], diff_lambda_q2[i],
                            diff_lambda_k2[i], diff_subln_g[i], mla_q_a_norm_g[i], mla_w_uq[i],
                            mla_kv_a_norm_g[i], mla_w_ukv[i], mla_q_norm_g[i], mla_k_norm_g[i],
                            _to_bf16(attn_w_out, i), batch=batch, seq=seq)
        else:
            h = _gated_block("conv", h, conv_norm_g[i].reshape(1, -1), (_to_bf16(conv_w_in, i),),
                             conv_w[i], None, _to_bf16(conv_w_out, i), seq=seq)
        h = _gated_block("ffn", h, ffn_norm_g[layer].reshape(1, -1),
                         (_to_bf16(ffn_w_gate, layer), _to_bf16(ffn_w_up, layer)),
                         ffn_dwconv_w[layer], ffn_dwconv_b[layer].reshape(1, -1),
                         _to_bf16(ffn_w_down, layer), seq=seq)
    return h.reshape(batch, seq, d)
```
